```python
import jax, jax.numpy as jnp
from jax import lax
import numpy as np

D_MODEL = 1024
BATCH = 4
SEQ = 4096
DEPTH = 2

D_FF = 2816
GDN_HEADS = 8
GDN_DK = 128
GDN_DV = 128
GDN_CONV = 4
CHUNK = 64
CNV_CH = 1024
CNV_K = 31
W_Q = GDN_HEADS * GDN_DK
W_K = GDN_HEADS * GDN_DK
W_V = GDN_HEADS * GDN_DV
W_Z = GDN_HEADS * GDN_DV
W_BETA = GDN_HEADS
W_A = GDN_HEADS
W_GLU = 2 * CNV_CH
W_GATE = 2 * D_MODEL
SPLITS = [int(s) for s in np.cumsum([W_Q + W_K + W_V, W_Z, W_BETA, W_A, W_GLU])]
P_IN = W_Q + W_K + W_V + W_Z + W_BETA + W_A + W_GLU + W_GATE
RMS_EPS = 1e-6
LN_EPS = 1e-5

kernel_name = "hybrid_gdn_conformer_macaron_sandwich"


def rmsnorm(x, w):
    xf = x.astype(jnp.float32)
    y = xf * lax.rsqrt(jnp.mean(xf * xf, axis=-1, keepdims=True) + RMS_EPS)
    return (y * w.astype(jnp.float32)).astype(x.dtype)


def layernorm(x, g, b):
    xf = x.astype(jnp.float32)
    mu = jnp.mean(xf, axis=-1, keepdims=True)
    var = jnp.mean(jnp.square(xf - mu), axis=-1, keepdims=True)
    y = (xf - mu) * lax.rsqrt(var + LN_EPS)
    return (y * g.astype(jnp.float32) + b.astype(jnp.float32)).astype(x.dtype)


def l2norm(x):
    return x * lax.rsqrt(jnp.sum(x * x, axis=-1, keepdims=True) + 1e-6)


def causal_dwconv(x, w):
    k, c = w.shape
    return lax.conv_general_dilated(
        x, w[:, None, :].astype(x.dtype), window_strides=(1,), padding=[(k - 1, 0)],
        dimension_numbers=("NWC", "WIO", "NWC"), feature_group_count=c)


def swiglu_ffn(h, w_in, w_out):
    gate, up = jnp.split(h @ w_in, 2, axis=-1)
    return (jax.nn.silu(gate) * up) @ w_out


def chunk_gated_delta_rule(q, k, v, g, beta):
    b, l, h, dk = q.shape
    dv = v.shape[-1]
    n = l // CHUNK

    def to_chunks(t):
        return t.reshape(b, n, CHUNK, h, -1).transpose(0, 3, 1, 2, 4)

    q, k, v = to_chunks(q), to_chunks(k), to_chunks(v)
    g = g.reshape(b, n, CHUNK, h).transpose(0, 3, 1, 2)
    beta = beta.reshape(b, n, CHUNK, h).transpose(0, 3, 1, 2)
    G = jnp.cumsum(g, axis=-1)
    causal = jnp.tril(jnp.ones((CHUNK, CHUNK), dtype=bool))
    strict = jnp.tril(jnp.ones((CHUNK, CHUNK), dtype=bool), k=-1)
    diff = G[..., :, None] - G[..., None, :]
    decay = jnp.where(causal, jnp.exp(jnp.where(causal, diff, 0.0)), 0.0)
    kk = jnp.einsum("bhncd,bhnsd->bhncs", k, k)
    a_mat = jnp.where(strict, kk * decay * beta[..., :, None], 0.0)
    lhs = jnp.eye(CHUNK, dtype=jnp.float32) + a_mat
    rhs = jnp.concatenate([v * beta[..., None], k * (beta * jnp.exp(G))[..., None]], axis=-1)
    sol = lax.linalg.triangular_solve(lhs, rhs, left_side=True, lower=True, unit_diagonal=True)
    u, w = sol[..., :dv], sol[..., dv:]
    qk = jnp.einsum("bhncd,bhnsd->bhncs", q, k) * decay
    q_dec = q * jnp.exp(G)[..., None]
    k_dec = k * jnp.exp(G[..., -1:] - G)[..., None]
    chunk_decay = jnp.exp(G[..., -1])

    xs = tuple(jnp.moveaxis(t, 2, 0) for t in (q_dec, k_dec, u, w, qk, chunk_decay))

    def step(state, inp):
        q_c, k_c, u_c, w_c, qk_c, d_c = inp
        v_new = u_c - jnp.einsum("bhck,bhkv->bhcv", w_c, state)
        o_c = jnp.einsum("bhck,bhkv->bhcv", q_c, state) + jnp.einsum("bhcs,bhsv->bhcv", qk_c, v_new)
        state = state * d_c[..., None, None] + jnp.einsum("bhck,bhcv->bhkv", k_c, v_new)
        return state, o_c

    s0 = jnp.zeros((b, h, dk, dv), jnp.float32)
    _, o = lax.scan(step, s0, xs)
    return o.transpose(1, 0, 3, 2, 4).reshape(b, l, h, dv)


def gated_deltanet(qkv, z, beta_logit, a_logit, conv_w, a_log, dt_bias, norm_w, w_o):
    b, l, _ = qkv.shape
    qkv = jax.nn.silu(causal_dwconv(qkv, conv_w))
    q, k, v = jnp.split(qkv.astype(jnp.float32), [W_Q, W_Q + W_K], axis=-1)
    q = l2norm(q.reshape(b, l, GDN_HEADS, GDN_DK)) * (GDN_DK ** -0.5)
    k = l2norm(k.reshape(b, l, GDN_HEADS, GDN_DK))
    v = v.reshape(b, l, GDN_HEADS, GDN_DV)
    beta = jax.nn.sigmoid(beta_logit.astype(jnp.float32))
    g = -jnp.exp(a_log.astype(jnp.float32)) * jax.nn.softplus(
        a_logit.astype(jnp.float32) + dt_bias.astype(jnp.float32))
    o = chunk_gated_delta_rule(q, k, v, g, beta)
    zf = z.astype(jnp.float32).reshape(b, l, GDN_HEADS, GDN_DV)
    o = rmsnorm(o, norm_w) * jax.nn.silu(zf)
    return o.reshape(b, l, GDN_HEADS * GDN_DV).astype(qkv.dtype) @ w_o


def conformer_conv(glu_in, pw1_b, dw_w, dw_b, ln_g, ln_b, w_o, b_o):
    a, gate = jnp.split(glu_in + pw1_b, 2, axis=-1)
    h = a * jax.nn.sigmoid(gate)
    h = causal_dwconv(h, dw_w) + dw_b
    h = jax.nn.silu(layernorm(h, ln_g, ln_b))
    return h @ w_o + b_o


def setup_inputs(seed: int = 0) -> dict:
    key = jax.random.key(seed)
    ks = iter(jax.random.split(key, 40))

    def nrm(shape, scale):
        return jax.random.normal(next(ks), shape, jnp.float32) * scale

    def gain(shape):
        return 1.0 + 0.02 * jax.random.normal(next(ks), shape, jnp.float32)

    L = DEPTH
    inp = {}
    inp["x"] = nrm((BATCH, SEQ, D_MODEL), 1.0)
    inp["ffn1_norm_pre"] = gain((L, D_MODEL))
    inp["ffn1_norm_post"] = gain((L, D_MODEL))
    inp["ffn1_w_in"] = nrm((L, D_MODEL, 2 * D_FF), D_MODEL ** -0.5)
    inp["ffn1_w_out"] = nrm((L, D_FF, D_MODEL), D_FF ** -0.5)
    inp["mix_norm_pre"] = gain((L, D_MODEL))
    inp["mix_norm_post"] = gain((L, D_MODEL))
    inp["mix_w_in"] = nrm((L, D_MODEL, P_IN), D_MODEL ** -0.5)
    inp["gdn_conv_w"] = nrm((L, GDN_CONV, W_Q + W_K + W_V), GDN_CONV ** -0.5)
    inp["gdn_a_log"] = jnp.log(jax.random.uniform(next(ks), (L, GDN_HEADS), jnp.float32, 1.0, 16.0))
    dt = jnp.exp(jax.random.uniform(next(ks), (L, GDN_HEADS), jnp.float32, np.log(1e-3), np.log(1e-1)))
    inp["gdn_dt_bias"] = dt + jnp.log(-jnp.expm1(-dt))
    inp["gdn_norm_w"] = gain((L, GDN_DV))
    inp["gdn_w_o"] = nrm((L, GDN_HEADS * GDN_DV, D_MODEL), (GDN_HEADS * GDN_DV) ** -0.5)
    inp["cnv_pw1_b"] = nrm((L, W_GLU), 0.02)
    inp["cnv_dw_w"] = nrm((L, CNV_K, CNV_CH), CNV_K ** -0.5)
    inp["cnv_dw_b"] = nrm((L, CNV_CH), 0.02)
    inp["cnv_ln_g"] = gain((L, CNV_CH))
    inp["cnv_ln_b"] = nrm((L, CNV_CH), 0.02)
    inp["cnv_w_o"] = nrm((L, CNV_CH, D_MODEL), CNV_CH ** -0.5)
    inp["cnv_b_o"] = nrm((L, D_MODEL), 0.02)
    inp["mix_w_out"] = nrm((L, D_MODEL, D_MODEL), D_MODEL ** -0.5)
    inp["ffn2_norm_pre"] = gain((L, D_MODEL))
    inp["ffn2_norm_post"] = gain((L, D_MODEL))
    inp["ffn2_w_in"] = nrm((L, D_MODEL, 2 * D_FF), D_MODEL ** -0.5)
    inp["ffn2_w_out"] = nrm((L, D_FF, D_MODEL), D_FF ** -0.5)
    return inp


def reference(x, ffn1_norm_pre, ffn1_norm_post, ffn1_w_in, ffn1_w_out,
              mix_norm_pre, mix_norm_post, mix_w_in,
              gdn_conv_w, gdn_a_log, gdn_dt_bias, gdn_norm_w, gdn_w_o,
              cnv_pw1_b, cnv_dw_w, cnv_dw_b, cnv_ln_g, cnv_ln_b, cnv_w_o, cnv_b_o,
              mix_w_out,
              ffn2_norm_pre, ffn2_norm_post, ffn2_w_in, ffn2_w_out):
    for i in range(DEPTH):
        f = swiglu_ffn(rmsnorm(x, ffn1_norm_pre[i]), ffn1_w_in[i], ffn1_w_out[i])
        x = x + 0.5 * rmsnorm(f, ffn1_norm_post[i])

        h = rmsnorm(x, mix_norm_pre[i])
        p = h @ mix_w_in[i]
        qkv, z, beta_logit, a_logit, glu_in, gates = jnp.split(p, SPLITS, axis=-1)
        y_a = gated_deltanet(qkv, z, beta_logit, a_logit, gdn_conv_w[i], gdn_a_log[i],
                             gdn_dt_bias[i], gdn_norm_w[i], gdn_w_o[i])
        y_b = conformer_conv(glu_in, cnv_pw1_b[i], cnv_dw_w[i], cnv_dw_b[i],
                             cnv_ln_g[i], cnv_ln_b[i], cnv_w_o[i], cnv_b_o[i])
        g_a, g_b = jnp.split(jax.nn.sigmoid(gates), 2, axis=-1)
        y = (g_a * y_a + g_b * y_b) @ mix_w_out[i]
        x = x + rmsnorm(y, mix_norm_post[i])

        f = swiglu_ffn(rmsnorm(x, ffn2_norm_pre[i]), ffn2_w_in[i], ffn2_w_out[i])
        x = x + 0.5 * rmsnorm(f, ffn2_norm_post[i])
    return x
```

```python
import functools
import math

import jax
import jax.numpy as jnp
from jax import lax
from jax.experimental import pallas as pl
from jax.experimental.pallas import tpu as pltpu

D_MODEL = 1024
D_FF = 2816
HEADS = 8
HEAD_DIM = 128
GDN_CONV = 4
CNV_CH = 1024
CNV_K = 31
RMS_EPS = 1e-6
LN_EPS = 1e-5

LANES = 128
SUBLANES = 8
VMEM_LIMIT_BYTES = 56 * 1024 * 1024

FFN_ROWS = 512
FF_CHUNK = 1408
MIX_ROWS = 256
OUT_ROWS = 512
CHUNK = 64
QKV_HALO = SUBLANES
GLU_HALO = 4 * SUBLANES
SMALL_COLS = LANES
SMALL_ROWS = 4 * SUBLANES

_BF16 = jnp.bfloat16
_F32 = jnp.float32


def _dot(a, b):
  return jnp.dot(a, b, preferred_element_type=_F32)


def _dot_nt(a, b):
  return lax.dot_general(a, b, (((1,), (1,)), ((), ())), preferred_element_type=_F32)


def _dot_tn(a, b):
  return lax.dot_general(a, b, (((0,), (0,)), ((), ())), preferred_element_type=_F32)


def _dot_exact(a, b):
  return jnp.dot(a, b, preferred_element_type=_F32, precision=lax.Precision.HIGHEST)


def _rms(x, w):
  ms = jnp.mean(x * x, axis=-1, keepdims=True)
  return x * lax.rsqrt(ms + RMS_EPS) * w


def _sigmoid(x):
  return 1.0 / (1.0 + jnp.exp(-x))


def _silu(x):
  return x * _sigmoid(x)


def _softplus(x):
  return jnp.maximum(x, 0.0) + jnp.log1p(jnp.exp(-jnp.abs(x)))


def _resident(shape):
  zeros = (0,) * len(shape)
  return pl.BlockSpec(shape, lambda *_: zeros, pipeline_mode=pl.Buffered(1))


def _ffn_body(x_ref, npre_ref, npost_ref, win_ref, wout_ref, o_ref):
  x = x_ref[...]
  h = _rms(x, npre_ref[...]).astype(_BF16)
  acc = jnp.zeros((FFN_ROWS, D_MODEL), _F32)
  for c in range(D_FF // FF_CHUNK):
    lo = c * FF_CHUNK
    gate = _dot(h, win_ref[:, lo:lo + FF_CHUNK])
    up = _dot(h, win_ref[:, D_FF + lo:D_FF + lo + FF_CHUNK])
    act = (_silu(gate) * up).astype(_BF16)
    acc = acc + _dot(act, wout_ref[lo:lo + FF_CHUNK, :])
  o_ref[...] = x + 0.5 * _rms(acc, npost_ref[...])


def _ffn(x2, npre, npost, w_in, w_out):
  t = x2.shape[0]
  row = pl.BlockSpec((FFN_ROWS, D_MODEL), lambda i: (i, 0))
  return pl.pallas_call(
      _ffn_body,
      grid=(t // FFN_ROWS,),
      in_specs=[row, _resident((1, D_MODEL)), _resident((1, D_MODEL)),
                _resident((D_MODEL, 2 * D_FF)), _resident((D_FF, D_MODEL))],
      out_specs=row,
      out_shape=jax.ShapeDtypeStruct((t, D_MODEL), _F32),
      compiler_params=pltpu.CompilerParams(
          dimension_semantics=("arbitrary",), vmem_limit_bytes=VMEM_LIMIT_BYTES),
      name="ffn",
  )(x2, npre, npost, w_in, w_out)


def _chunk_masks(n):
  shift = int(math.log2(CHUNK))
  row = lax.broadcasted_iota(jnp.int32, (n, n), 0)
  col = lax.broadcasted_iota(jnp.int32, (n, n), 1)
  same = lax.shift_right_logical(row, shift) == lax.shift_right_logical(col, shift)
  return same & (col <= row), same


def _mix_in_body(x_ref, npre_ref, wqkv_ref, wz_ref, wglu_ref, wgat_ref, wsm_ref, wsmt_ref,
                 convw_ref, smc_ref, smct_ref, pw1b_ref, dww_ref, dwb_ref, lng_ref, lnb_ref,
                 cwo_ref, cbo_ref,
                 qkv_ref, gz_ref, ga_ref, ybg_ref, bg_ref, bgt_ref,
                 qkv_ext, glu_ext, conv_out):
  n = MIX_ROWS
  t = pl.program_id(1)

  @pl.when(t == 0)
  def _():
    qkv_ext[:, 0:QKV_HALO, :] = jnp.zeros((3 * HEADS, QKV_HALO, LANES), _F32)
    glu_ext[:, 0:GLU_HALO, :] = jnp.zeros((CNV_CH // LANES, GLU_HALO, LANES), _F32)

  h = _rms(x_ref[...], npre_ref[...]).astype(_BF16)

  for part in range(3):
    p = _dot(h, wqkv_ref[:, part * D_MODEL:(part + 1) * D_MODEL])
    for c in range(HEADS):
      qkv_ext[part * HEADS + c, QKV_HALO:QKV_HALO + n, :] = p[:, c * LANES:(c + 1) * LANES]

  def qkv_tile(c, mode):
    w = convw_ref[c]
    acc = jnp.zeros((n, LANES), _F32)
    for j in range(GDN_CONV):
      start = QKV_HALO - (GDN_CONV - 1) + j
      acc = acc + w[j:j + 1, :] * qkv_ext[c, pl.ds(start, n), :]
    y = _silu(acc)
    if mode != "v":
      y = y * lax.rsqrt(jnp.sum(y * y, axis=-1, keepdims=True) + 1e-6)
    if mode == "q":
      y = y * (HEAD_DIM ** -0.5)
    qkv_ref[c] = y
    qkv_ext[c, 0:QKV_HALO, :] = qkv_ext[c, n:n + QKV_HALO, :]

  def run(mode, lo):
    def body(c, carry):
      qkv_tile(c, mode)
      return carry
    lax.fori_loop(lo, lo + HEADS, body, 0)

  run("q", 0)
  run("k", HEADS)
  run("v", 2 * HEADS)

  gz_ref[...] = _silu(_dot(h, wz_ref[...]))

  glu = _dot(h, wglu_ref[...]) + pw1b_ref[...]
  hh = glu[:, :CNV_CH] * _sigmoid(glu[:, CNV_CH:])
  for c in range(CNV_CH // LANES):
    glu_ext[c, GLU_HALO:GLU_HALO + n, :] = hh[:, c * LANES:(c + 1) * LANES]

  def glu_tile(c, carry):
    w = dww_ref[c]
    acc = jnp.zeros((n, LANES), _F32)
    for j in range(CNV_K):
      start = GLU_HALO - (CNV_K - 1) + j
      acc = acc + w[j:j + 1, :] * glu_ext[c, pl.ds(start, n), :]
    conv_out[c] = acc
    glu_ext[c, 0:GLU_HALO, :] = glu_ext[c, n:n + GLU_HALO, :]
    return carry

  lax.fori_loop(0, CNV_CH // LANES, glu_tile, 0)
  cv = jnp.concatenate([conv_out[c] for c in range(CNV_CH // LANES)], axis=-1) + dwb_ref[...]
  mu = jnp.mean(cv, axis=-1, keepdims=True)
  cen = cv - mu
  var = jnp.mean(cen * cen, axis=-1, keepdims=True)
  ln = cen * lax.rsqrt(var + LN_EPS) * lng_ref[...] + lnb_ref[...]
  y_b = _dot(_silu(ln).astype(_BF16), cwo_ref[...]) + cbo_ref[...]

  gates = _sigmoid(_dot(h, wgat_ref[...]))
  ga_ref[...] = gates[:, :D_MODEL]
  ybg_ref[...] = gates[:, D_MODEL:] * y_b

  causal, same = _chunk_masks(n)
  lower = causal.astype(_F32)
  block = same.astype(_F32)
  sm = _dot(h, wsm_ref[...])
  a_log = smc_ref[0:1, :]
  dt_bias = smc_ref[1:2, :]
  g = -jnp.exp(a_log) * _softplus(sm + dt_bias)
  lane = lax.broadcasted_iota(jnp.int32, (n, SMALL_COLS), 1)
  bg_ref[...] = jnp.where(lane < HEADS, _sigmoid(sm),
                          jnp.where(lane < 2 * HEADS, _dot_exact(lower, g), _dot_exact(block, g)))
  smt = _dot_nt(wsmt_ref[...], h)
  gt = -jnp.exp(smct_ref[:, 0:1]) * _softplus(smt + smct_ref[:, 1:2])
  upper = (lax.broadcasted_iota(jnp.int32, (n, n), 0) <= lax.broadcasted_iota(jnp.int32, (n, n), 1))
  upper = (upper & same).astype(_F32)
  sub = lax.broadcasted_iota(jnp.int32, (SMALL_ROWS, n), 0)
  bgt_ref[...] = jnp.where(sub < HEADS, _sigmoid(smt), _dot_exact(gt, upper))


def _mix_in(x2, npre, wqkv, wz, wglu, wgat, wsm, wsmt, convw, smc, smct, pw1b, dww, dwb,
            lng, lnb, cwo, cbo, batch, seq):
  t = x2.shape[0]
  nt = seq // MIX_ROWS
  row = lambda width: pl.BlockSpec((MIX_ROWS, width), lambda b, i: (b * nt + i, 0))
  weights = [npre, wqkv, wz, wglu, wgat, wsm, wsmt, convw, smc, smct, pw1b, dww, dwb, lng, lnb,
             cwo, cbo]
  out_shape = (
      jax.ShapeDtypeStruct((3 * HEADS, t, LANES), _F32),
      jax.ShapeDtypeStruct((t, D_MODEL), _F32),
      jax.ShapeDtypeStruct((t, D_MODEL), _F32),
      jax.ShapeDtypeStruct((t, D_MODEL), _F32),
      jax.ShapeDtypeStruct((t, SMALL_COLS), _F32),
      jax.ShapeDtypeStruct((SMALL_ROWS, t), _F32),
  )
  out_specs = (
      pl.BlockSpec((3 * HEADS, MIX_ROWS, LANES), lambda b, i: (0, b * nt + i, 0)),
      row(D_MODEL), row(D_MODEL), row(D_MODEL), row(SMALL_COLS),
      pl.BlockSpec((SMALL_ROWS, MIX_ROWS), lambda b, i: (0, b * nt + i)),
  )
  return pl.pallas_call(
      _mix_in_body,
      grid=(batch, nt),
      in_specs=[row(D_MODEL)] + [_resident(w.shape) for w in weights],
      out_specs=out_specs,
      out_shape=out_shape,
      scratch_shapes=[
          pltpu.VMEM((3 * HEADS, QKV_HALO + MIX_ROWS, LANES), _F32),
          pltpu.VMEM((CNV_CH // LANES, GLU_HALO + MIX_ROWS, LANES), _F32),
          pltpu.VMEM((CNV_CH // LANES, MIX_ROWS, LANES), _F32),
      ],
      compiler_params=pltpu.CompilerParams(
          dimension_semantics=("arbitrary", "arbitrary"), vmem_limit_bytes=VMEM_LIMIT_BYTES),
      name="mix_in",
  )(x2, *weights)


def _gdn_body(q_ref, k_ref, v_ref, gz_ref, bg_ref, bgt_ref, nw_ref, o_ref, state, vnew):
  n = MIX_ROWS
  t = pl.program_id(1)
  hd = pl.program_id(2)

  @pl.when(t == 0)
  def _():
    state[hd] = jnp.zeros((HEAD_DIM, HEAD_DIM), _F32)

  q = q_ref[0]
  k = k_ref[0]
  v = v_ref[0]
  bg = bg_ref[...]
  lane = lax.broadcasted_iota(jnp.int32, (n, SMALL_COLS), 1)

  def column(idx):
    return jnp.sum(jnp.where(lane == idx, bg, 0.0), axis=-1, keepdims=True)

  beta = column(hd)
  g_col = column(hd + HEADS)
  g_last = column(hd + 2 * HEADS)
  g_row = bgt_ref[pl.ds(hd + HEADS, 1), :]

  causal, _ = _chunk_masks(n)
  row = lax.broadcasted_iota(jnp.int32, (n, n), 0)
  col = lax.broadcasted_iota(jnp.int32, (n, n), 1)
  strict = causal & (col < row)
  decay = jnp.where(causal, jnp.exp(jnp.where(causal, g_col - g_row, 0.0)), 0.0)

  kb = k.astype(_BF16)
  qk = _dot_nt(q.astype(_BF16), kb) * decay
  kk = _dot_nt(kb, kb)
  a_mat = jnp.where(strict, kk * decay * beta, 0.0)

  eye = (row == col).astype(_F32)
  x = eye - a_mat
  p = a_mat
  for _ in range(int(math.log2(CHUNK)) - 1):
    pb = p.astype(_BF16)
    p = _dot(pb, pb)
    x = x + _dot(x.astype(_BF16), p.astype(_BF16))

  e_g = jnp.exp(g_col)
  rhs = jnp.concatenate([v * beta, k * (beta * e_g)], axis=-1).astype(_BF16)
  uw = _dot(x.astype(_BF16), rhs)
  u = uw[:, :HEAD_DIM]
  w = uw[:, HEAD_DIM:]
  q_dec = q * e_g
  k_dec = k * jnp.exp(g_last - g_col)
  qkb = qk.astype(_BF16)

  vnew[...] = jnp.zeros((n, HEAD_DIM), _BF16)
  s = state[hd]
  for c in range(n // CHUNK):
    r = slice(c * CHUNK, (c + 1) * CHUNK)
    sb = s.astype(_BF16)
    wq = jnp.concatenate([w[r], q_dec[r]], axis=0).astype(_BF16)
    ws_qs = _dot(wq, sb)
    v_new = u[r] - ws_qs[:CHUNK]
    vnew[r, :] = v_new.astype(_BF16)
    o_c = ws_qs[CHUNK:] + _dot(qkb[r, :], vnew[...])
    d_c = jnp.exp(g_last[c * CHUNK:c * CHUNK + 1, :])
    s = s * d_c + _dot_tn(k_dec[r].astype(_BF16), v_new.astype(_BF16))
    o_n = _rms(o_c, nw_ref[...]) * gz_ref[r, :]
    o_ref[r, :] = o_n.astype(o_ref.dtype)
  state[hd] = s


def _gdn(qkvh, gz, bg, bgt, norm_w, batch, seq):
  t = gz.shape[0]
  nt = seq // MIX_ROWS

  def head_major(offset):
    return pl.BlockSpec((1, MIX_ROWS, LANES), lambda b, i, h: (offset + h, b * nt + i, 0))

  tok = pl.BlockSpec((MIX_ROWS, LANES), lambda b, i, h: (b * nt + i, h))
  return pl.pallas_call(
      _gdn_body,
      grid=(batch, nt, HEADS),
      in_specs=[head_major(0), head_major(HEADS), head_major(2 * HEADS), tok,
                pl.BlockSpec((MIX_ROWS, SMALL_COLS), lambda b, i, h: (b * nt + i, 0)),
                pl.BlockSpec((SMALL_ROWS, MIX_ROWS), lambda b, i, h: (0, b * nt + i)),
                pl.BlockSpec((1, HEAD_DIM), lambda b, i, h: (0, 0))],
      out_specs=tok,
      out_shape=jax.ShapeDtypeStruct((t, D_MODEL), _BF16),
      scratch_shapes=[pltpu.VMEM((HEADS, HEAD_DIM, HEAD_DIM), _F32),
                      pltpu.VMEM((MIX_ROWS, HEAD_DIM), _BF16)],
      compiler_params=pltpu.CompilerParams(
          dimension_semantics=("arbitrary", "arbitrary", "arbitrary"),
          vmem_limit_bytes=VMEM_LIMIT_BYTES),
      name="gdn",
  )(qkvh, qkvh, qkvh, gz, bg, bgt, norm_w)


def _mix_out_body(x_ref, og_ref, ga_ref, ybg_ref, wgo_ref, wmo_ref, npost_ref, o_ref):
  y_a = _dot(og_ref[...], wgo_ref[...])
  y = ga_ref[...] * y_a + ybg_ref[...]
  y2 = _dot(y.astype(_BF16), wmo_ref[...])
  o_ref[...] = x_ref[...] + _rms(y2, npost_ref[...])


def _mix_out(x2, og, ga, ybg, wgo, wmo, npost):
  t = x2.shape[0]
  row = pl.BlockSpec((OUT_ROWS, D_MODEL), lambda i: (i, 0))
  return pl.pallas_call(
      _mix_out_body,
      grid=(t // OUT_ROWS,),
      in_specs=[row, row, row, row, _resident((D_MODEL, D_MODEL)), _resident((D_MODEL, D_MODEL)),
                _resident((1, D_MODEL))],
      out_specs=row,
      out_shape=jax.ShapeDtypeStruct((t, D_MODEL), _F32),
      compiler_params=pltpu.CompilerParams(
          dimension_semantics=("arbitrary",), vmem_limit_bytes=VMEM_LIMIT_BYTES),
      name="mix_out",
  )(x2, og, ga, ybg, wgo, wmo, npost)


def _row(v):
  return v.reshape(1, -1).astype(_F32)


def _lane_tiles(w):
  k, c = w.shape
  return w.reshape(k, c // LANES, LANES).transpose(1, 0, 2).astype(_F32)


def kernel(x, ffn1_norm_pre, ffn1_norm_post, ffn1_w_in, ffn1_w_out, mix_norm_pre, mix_norm_post,
           mix_w_in, gdn_conv_w, gdn_a_log, gdn_dt_bias, gdn_norm_w, gdn_w_o, cnv_pw1_b, cnv_dw_w,
           cnv_dw_b, cnv_ln_g, cnv_ln_b, cnv_w_o, cnv_b_o, mix_w_out, ffn2_norm_pre,
           ffn2_norm_post, ffn2_w_in, ffn2_w_out):
  batch, seq, _ = x.shape
  depth = ffn1_w_in.shape[0]
  x2 = x.reshape(batch * seq, D_MODEL)
  o_qkv = 3 * HEADS * HEAD_DIM
  o_z = o_qkv + HEADS * HEAD_DIM
  o_b = o_z + HEADS
  o_a = o_b + HEADS
  o_glu = o_a + 2 * CNV_CH
  for i in range(depth):
    x2 = _ffn(x2, _row(ffn1_norm_pre[i]), _row(ffn1_norm_post[i]),
              ffn1_w_in[i].astype(_BF16), ffn1_w_out[i].astype(_BF16))

    w = mix_w_in[i]
    w_beta = w[:, o_z:o_b]
    w_a = w[:, o_b:o_a]
    wsm = jnp.concatenate(
        [w_beta, w_a, w_a, jnp.zeros((D_MODEL, SMALL_COLS - 3 * HEADS), _F32)], axis=1)
    wsmt = jnp.concatenate(
        [w_beta, w_a, w_a, jnp.zeros((D_MODEL, SMALL_ROWS - 3 * HEADS), _F32)], axis=1).T
    pad = jnp.zeros((HEADS,), _F32)
    a_log = jnp.concatenate([pad, gdn_a_log[i], gdn_a_log[i]])
    dt_b = jnp.concatenate([pad, gdn_dt_bias[i], gdn_dt_bias[i]])
    smc = jnp.zeros((SUBLANES, SMALL_COLS), _F32)
    smc = smc.at[0, :3 * HEADS].set(a_log).at[1, :3 * HEADS].set(dt_b)
    smct = jnp.zeros((SMALL_ROWS, LANES), _F32)
    smct = smct.at[:3 * HEADS, 0].set(a_log).at[:3 * HEADS, 1].set(dt_b)

    qkvh, gz, ga, ybg, bg, bgt = _mix_in(
        x2, _row(mix_norm_pre[i]), w[:, :o_qkv].astype(_BF16), w[:, o_qkv:o_z].astype(_BF16),
        w[:, o_a:o_glu].astype(_BF16), w[:, o_glu:].astype(_BF16), wsm.astype(_BF16),
        wsmt.astype(_BF16), _lane_tiles(gdn_conv_w[i]), smc, smct, _row(cnv_pw1_b[i]),
        _lane_tiles(cnv_dw_w[i]), _row(cnv_dw_b[i]), _row(cnv_ln_g[i]), _row(cnv_ln_b[i]),
        cnv_w_o[i].astype(_BF16), _row(cnv_b_o[i]), batch, seq)
    og = _gdn(qkvh, gz, bg, bgt, _row(gdn_norm_w[i]), batch, seq)
    x2 = _mix_out(x2, og, ga, ybg, gdn_w_o[i].astype(_BF16), mix_w_out[i].astype(_BF16),
                  _row(mix_norm_post[i]))

    x2 = _ffn(x2, _row(ffn2_norm_pre[i]), _row(ffn2_norm_post[i]),
              ffn2_w_in[i].astype(_BF16), ffn2_w_out[i].astype(_BF16))
  return x2.reshape(batch, seq, D_MODEL)
```

```python
import functools
import math

import jax
import jax.numpy as jnp
from jax import lax
from jax.experimental import pallas as pl
from jax.experimental.pallas import tpu as pltpu

D_MODEL = 1024
D_FF = 2816
HEADS = 8
HEAD_DIM = 128
GDN_CONV = 4
CNV_CH = 1024
CNV_K = 31
RMS_EPS = 1e-6
LN_EPS = 1e-5

LANES = 128
SUBLANES = 8
VMEM_LIMIT_BYTES = 56 * 1024 * 1024

FFN_ROWS = 512
FF_CHUNK = 1408
MIX_ROWS = 256
MIX_PIECE = 512
OUT_ROWS = 512
CHUNK = 64
GDN_GROUP = 8
QKV_HALO = SUBLANES
GLU_HALO = 4 * SUBLANES
SMALL_COLS = LANES
SMALL_ROWS = 4 * SUBLANES

_BF16 = jnp.bfloat16
_F32 = jnp.float32


def _dot(a, b):
  return jnp.dot(a, b, preferred_element_type=_F32)


def _dot_nt(a, b):
  return lax.dot_general(a, b, (((1,), (1,)), ((), ())), preferred_element_type=_F32)


def _dot_tn(a, b):
  return lax.dot_general(a, b, (((0,), (0,)), ((), ())), preferred_element_type=_F32)


def _dot_exact(a, b):
  return jnp.dot(a, b, preferred_element_type=_F32, precision=lax.Precision.HIGHEST)


def _rms(x, w):
  ms = jnp.mean(x * x, axis=-1, keepdims=True)
  return x * lax.rsqrt(ms + RMS_EPS) * w


def _sigmoid(x):
  return 1.0 / (1.0 + jnp.exp(-x))


def _silu(x):
  return x * _sigmoid(x)


def _softplus(x):
  return jnp.maximum(x, 0.0) + jnp.log1p(jnp.exp(-jnp.abs(x)))


def _resident(shape):
  zeros = (0,) * len(shape)
  return pl.BlockSpec(shape, lambda *_: zeros, pipeline_mode=pl.Buffered(1))


def _ffn_body(x_ref, npre_ref, npost_ref, win_ref, wout_ref, o_ref):
  x = x_ref[...]
  h = _rms(x, npre_ref[...]).astype(_BF16)
  acc = jnp.zeros((FFN_ROWS, D_MODEL), _F32)
  for c in range(D_FF // FF_CHUNK):
    lo = c * FF_CHUNK
    gate = _dot(h, win_ref[:, lo:lo + FF_CHUNK])
    up = _dot(h, win_ref[:, D_FF + lo:D_FF + lo + FF_CHUNK])
    act = (_silu(gate) * up).astype(_BF16)
    acc = acc + _dot(act, wout_ref[lo:lo + FF_CHUNK, :])
  o_ref[...] = x + 0.5 * _rms(acc, npost_ref[...])


def _ffn(x2, npre, npost, w_in, w_out):
  t = x2.shape[0]
  row = pl.BlockSpec((FFN_ROWS, D_MODEL), lambda i: (i, 0))
  return pl.pallas_call(
      _ffn_body,
      grid=(t // FFN_ROWS,),
      in_specs=[row, _resident((1, D_MODEL)), _resident((1, D_MODEL)),
                _resident((D_MODEL, 2 * D_FF)), _resident((D_FF, D_MODEL))],
      out_specs=row,
      out_shape=jax.ShapeDtypeStruct((t, D_MODEL), _F32),
      compiler_params=pltpu.CompilerParams(
          dimension_semantics=("arbitrary",), vmem_limit_bytes=VMEM_LIMIT_BYTES),
      name="ffn",
  )(x2, npre, npost, w_in, w_out)


def _chunk_masks(n):
  shift = int(math.log2(CHUNK))
  row = lax.broadcasted_iota(jnp.int32, (n, n), 0)
  col = lax.broadcasted_iota(jnp.int32, (n, n), 1)
  same = lax.shift_right_logical(row, shift) == lax.shift_right_logical(col, shift)
  return same & (col <= row), same


def _mix_in_body(x_ref, npre_ref, wqkv_ref, wz_ref, wglu_ref, wgat_ref, wsm_ref, wsmt_ref,
                 convw_ref, smc_ref, smct_ref, pw1b_ref, dww_ref, dwb_ref, lng_ref, lnb_ref,
                 cwo_ref, cbo_ref,
                 qkv_ref, gz_ref, ga_ref, ybg_ref, bg_ref, bgt_ref,
                 qkv_ext, glu_ext):
  n = MIX_ROWS
  t = pl.program_id(1)

  @pl.when(t == 0)
  def _():
    qkv_ext[:, 0:QKV_HALO, :] = jnp.zeros((3 * HEADS, QKV_HALO, LANES), _F32)
    glu_ext[:, 0:GLU_HALO, :] = jnp.zeros((CNV_CH // LANES, GLU_HALO, LANES), _F32)

  h = _rms(x_ref[...], npre_ref[...]).astype(_BF16)

  tiles_per_piece = MIX_PIECE // LANES

  def project(part, half):
    lo = part * D_MODEL + half * MIX_PIECE
    p = _dot(h, wqkv_ref[:, lo:lo + MIX_PIECE])
    for c in range(tiles_per_piece):
      tile = lo // LANES + c
      qkv_ext[tile, QKV_HALO:QKV_HALO + n, :] = p[:, c * LANES:(c + 1) * LANES]

  def conv_qkv(part, half):
    first = part * HEADS + half * tiles_per_piece
    for c in range(first, first + tiles_per_piece):
      w = convw_ref[c]
      acc = jnp.zeros((n, LANES), _F32)
      for j in range(GDN_CONV):
        start = QKV_HALO - (GDN_CONV - 1) + j
        acc = acc + w[j:j + 1, :] * qkv_ext[c, start:start + n, :]
      y = _silu(acc)
      if part < 2:
        y = y * lax.rsqrt(jnp.sum(y * y, axis=-1, keepdims=True) + 1e-6)
      if part == 0:
        y = y * (HEAD_DIM ** -0.5)
      qkv_ref[c] = y
      qkv_ext[c, 0:QKV_HALO, :] = qkv_ext[c, n:n + QKV_HALO, :]

  conv_tiles = [None] * (CNV_CH // LANES)

  def conv_glu(c):
    w = dww_ref[c]
    acc = jnp.zeros((n, LANES), _F32)
    for j in range(CNV_K):
      start = GLU_HALO - (CNV_K - 1) + j
      acc = acc + w[j:j + 1, :] * glu_ext[c, start:start + n, :]
    conv_tiles[c] = acc
    glu_ext[c, 0:GLU_HALO, :] = glu_ext[c, n:n + GLU_HALO, :]

  def gate_z(half):
    lo = half * MIX_PIECE
    gz_ref[:, lo:lo + MIX_PIECE] = _silu(_dot(h, wz_ref[:, lo:lo + MIX_PIECE]))

  def gate_piece(idx):
    lo = idx * MIX_PIECE
    return _sigmoid(_dot(h, wgat_ref[:, lo:lo + MIX_PIECE]))

  for half in range(2):
    lo = half * MIX_PIECE
    a = _dot(h, wglu_ref[:, lo:lo + MIX_PIECE]) + pw1b_ref[:, lo:lo + MIX_PIECE]
    g = _dot(h, wglu_ref[:, CNV_CH + lo:CNV_CH + lo + MIX_PIECE])
    g = g + pw1b_ref[:, CNV_CH + lo:CNV_CH + lo + MIX_PIECE]
    hh = a * _sigmoid(g)
    for c in range(tiles_per_piece):
      glu_ext[half * tiles_per_piece + c, GLU_HALO:GLU_HALO + n, :] = hh[:, c * LANES:(c + 1) * LANES]

  project(0, 0)
  conv_glu(0)
  project(0, 1)
  conv_glu(1)
  project(1, 0)
  conv_glu(2)
  conv_qkv(0, 0)
  project(1, 1)
  conv_glu(3)
  conv_qkv(0, 1)
  project(2, 0)
  conv_glu(4)
  conv_qkv(1, 0)
  project(2, 1)
  conv_glu(5)
  conv_qkv(1, 1)
  gate_z(0)
  conv_glu(6)
  conv_qkv(2, 0)
  gate_z(1)
  conv_glu(7)
  conv_qkv(2, 1)

  g_a0 = gate_piece(0)
  cv = jnp.concatenate(conv_tiles, axis=-1) + dwb_ref[...]
  mu = jnp.mean(cv, axis=-1, keepdims=True)
  cen = cv - mu
  var = jnp.mean(cen * cen, axis=-1, keepdims=True)
  ln = cen * lax.rsqrt(var + LN_EPS) * lng_ref[...] + lnb_ref[...]
  act_b = _silu(ln).astype(_BF16)
  g_a1 = gate_piece(1)
  ga_ref[:, 0:MIX_PIECE] = g_a0
  ga_ref[:, MIX_PIECE:2 * MIX_PIECE] = g_a1
  for half in range(2):
    lo = half * MIX_PIECE
    g_b = gate_piece(2 + half)
    y_b = _dot(act_b, cwo_ref[:, lo:lo + MIX_PIECE]) + cbo_ref[:, lo:lo + MIX_PIECE]
    ybg_ref[:, lo:lo + MIX_PIECE] = g_b * y_b

  causal, same = _chunk_masks(n)
  lower = causal.astype(_F32)
  block = same.astype(_F32)
  sm = _dot(h, wsm_ref[...])
  a_log = smc_ref[0:1, :]
  dt_bias = smc_ref[1:2, :]
  g = -jnp.exp(a_log) * _softplus(sm + dt_bias)
  lane = lax.broadcasted_iota(jnp.int32, (n, SMALL_COLS), 1)
  bg_ref[...] = jnp.where(lane < HEADS, _sigmoid(sm),
                          jnp.where(lane < 2 * HEADS, _dot_exact(lower, g), _dot_exact(block, g)))
  smt = _dot_nt(wsmt_ref[...], h)
  gt = -jnp.exp(smct_ref[:, 0:1]) * _softplus(smt + smct_ref[:, 1:2])
  upper = (lax.broadcasted_iota(jnp.int32, (n, n), 0) <= lax.broadcasted_iota(jnp.int32, (n, n), 1))
  upper = (upper & same).astype(_F32)
  sub = lax.broadcasted_iota(jnp.int32, (SMALL_ROWS, n), 0)
  bgt_ref[...] = jnp.where(sub < HEADS, _sigmoid(smt), _dot_exact(gt, upper))


def _mix_in(x2, npre, wqkv, wz, wglu, wgat, wsm, wsmt, convw, smc, smct, pw1b, dww, dwb,
            lng, lnb, cwo, cbo, batch, seq):
  t = x2.shape[0]
  nt = seq // MIX_ROWS
  row = lambda width: pl.BlockSpec((MIX_ROWS, width), lambda b, i: (b * nt + i, 0))
  weights = [npre, wqkv, wz, wglu, wgat, wsm, wsmt, convw, smc, smct, pw1b, dww, dwb, lng, lnb,
             cwo, cbo]
  out_shape = (
      jax.ShapeDtypeStruct((3 * HEADS, t, LANES), _F32),
      jax.ShapeDtypeStruct((t, D_MODEL), _F32),
      jax.ShapeDtypeStruct((t, D_MODEL), _F32),
      jax.ShapeDtypeStruct((t, D_MODEL), _F32),
      jax.ShapeDtypeStruct((t, SMALL_COLS), _F32),
      jax.ShapeDtypeStruct((SMALL_ROWS, t), _F32),
  )
  out_specs = (
      pl.BlockSpec((3 * HEADS, MIX_ROWS, LANES), lambda b, i: (0, b * nt + i, 0)),
      row(D_MODEL), row(D_MODEL), row(D_MODEL), row(SMALL_COLS),
      pl.BlockSpec((SMALL_ROWS, MIX_ROWS), lambda b, i: (0, b * nt + i)),
  )
  return pl.pallas_call(
      _mix_in_body,
      grid=(batch, nt),
      in_specs=[row(D_MODEL)] + [_resident(w.shape) for w in weights],
      out_specs=out_specs,
      out_shape=out_shape,
      scratch_shapes=[
          pltpu.VMEM((3 * HEADS, QKV_HALO + MIX_ROWS, LANES), _F32),
          pltpu.VMEM((CNV_CH // LANES, GLU_HALO + MIX_ROWS, LANES), _F32),
      ],
      compiler_params=pltpu.CompilerParams(
          dimension_semantics=("arbitrary", "arbitrary"), vmem_limit_bytes=VMEM_LIMIT_BYTES),
      name="mix_in",
  )(x2, *weights)


def _gdn_body(q_ref, k_ref, v_ref, gz_ref, bg_ref, bgt_ref, nw_ref, o_ref, state):
  n = MIX_ROWS
  t = pl.program_id(1)
  group = pl.program_id(2)
  heads = range(GDN_GROUP)

  @pl.when(t == 0)
  def _():
    for j in heads:
      state[group * GDN_GROUP + j] = jnp.zeros((HEAD_DIM, HEAD_DIM), _F32)

  bg = bg_ref[...]
  lane = lax.broadcasted_iota(jnp.int32, (n, SMALL_COLS), 1)

  def column(idx):
    return jnp.sum(jnp.where(lane == idx, bg, 0.0), axis=-1, keepdims=True)

  causal, _ = _chunk_masks(n)
  row = lax.broadcasted_iota(jnp.int32, (n, n), 0)
  col = lax.broadcasted_iota(jnp.int32, (n, n), 1)
  strict = causal & (col < row)
  eye = (row == col).astype(_F32)

  hd = [group * GDN_GROUP + j for j in heads]
  beta = [column(hd[j]) for j in heads]
  g_col = [column(hd[j] + HEADS) for j in heads]
  g_last = [column(hd[j] + 2 * HEADS) for j in heads]
  decay, qkb, xs, ps = [], [], [], []
  for j in heads:
    g_row = bgt_ref[pl.ds(hd[j] + HEADS, 1), :]
    d = jnp.where(causal, jnp.exp(jnp.where(causal, g_col[j] - g_row, 0.0)), 0.0)
    kb = k_ref[j].astype(_BF16)
    qk = (_dot_nt(q_ref[j].astype(_BF16), kb) * d).astype(_BF16)
    qkb.append([qk[c * CHUNK:(c + 1) * CHUNK, c * CHUNK:(c + 1) * CHUNK]
                for c in range(n // CHUNK)])
    a_mat = jnp.where(strict, _dot_nt(kb, kb) * d * beta[j], 0.0)
    xs.append(eye - a_mat)
    ps.append(a_mat)

  for _ in range(int(math.log2(CHUNK)) - 1):
    for j in heads:
      pb = ps[j].astype(_BF16)
      ps[j] = _dot(pb, pb)
      xs[j] = xs[j] + _dot(xs[j].astype(_BF16), ps[j].astype(_BF16))

  us, ws, q_dec, k_dec = [], [], [], []
  for j in heads:
    k = k_ref[j]
    e_g = jnp.exp(g_col[j])
    rhs = jnp.concatenate([v_ref[j] * beta[j], k * (beta[j] * e_g)], axis=-1).astype(_BF16)
    uw = _dot(xs[j].astype(_BF16), rhs)
    us.append(uw[:, :HEAD_DIM])
    ws.append(uw[:, HEAD_DIM:])
    q_dec.append(q_ref[j] * e_g)
    k_dec.append((k * jnp.exp(g_last[j] - g_col[j])).astype(_BF16))

  s = [state[hd[j]] for j in heads]
  for c in range(n // CHUNK):
    r = slice(c * CHUNK, (c + 1) * CHUNK)
    ws_qs = []
    for j in heads:
      wq = jnp.concatenate([ws[j][r], q_dec[j][r]], axis=0).astype(_BF16)
      ws_qs.append(_dot(wq, s[j].astype(_BF16)))
    for j in heads:
      v_new = (us[j][r] - ws_qs[j][:CHUNK]).astype(_BF16)
      o_c = ws_qs[j][CHUNK:] + _dot(qkb[j][c], v_new)
      d_c = jnp.exp(g_last[j][c * CHUNK:c * CHUNK + 1, :])
      s[j] = s[j] * d_c + _dot_tn(k_dec[j][r], v_new)
      o_n = _rms(o_c, nw_ref[...]) * gz_ref[r, j * HEAD_DIM:(j + 1) * HEAD_DIM]
      o_ref[r, j * HEAD_DIM:(j + 1) * HEAD_DIM] = o_n.astype(o_ref.dtype)
  for j in heads:
    state[hd[j]] = s[j]


def _gdn(qkvh, gz, bg, bgt, norm_w, batch, seq):
  t = gz.shape[0]
  nt = seq // MIX_ROWS
  groups = HEADS // GDN_GROUP

  def head_major(part):
    return pl.BlockSpec((GDN_GROUP, MIX_ROWS, LANES),
                        lambda b, i, g: (part * groups + g, b * nt + i, 0))

  tok = pl.BlockSpec((MIX_ROWS, GDN_GROUP * HEAD_DIM), lambda b, i, g: (b * nt + i, g))
  return pl.pallas_call(
      _gdn_body,
      grid=(batch, nt, groups),
      in_specs=[head_major(0), head_major(1), head_major(2), tok,
                pl.BlockSpec((MIX_ROWS, SMALL_COLS), lambda b, i, g: (b * nt + i, 0)),
                pl.BlockSpec((SMALL_ROWS, MIX_ROWS), lambda b, i, g: (0, b * nt + i)),
                pl.BlockSpec((1, HEAD_DIM), lambda b, i, g: (0, 0))],
      out_specs=tok,
      out_shape=jax.ShapeDtypeStruct((t, D_MODEL), _BF16),
      scratch_shapes=[pltpu.VMEM((HEADS, HEAD_DIM, HEAD_DIM), _F32)],
      compiler_params=pltpu.CompilerParams(
          dimension_semantics=("arbitrary", "arbitrary", "arbitrary"),
          vmem_limit_bytes=VMEM_LIMIT_BYTES),
      name="gdn",
  )(qkvh, qkvh, qkvh, gz, bg, bgt, norm_w)


def _mix_out_body(x_ref, og_ref, ga_ref, ybg_ref, wgo_ref, wmo_ref, npost_ref, o_ref):
  y_a = _dot(og_ref[...], wgo_ref[...])
  y = ga_ref[...] * y_a + ybg_ref[...]
  y2 = _dot(y.astype(_BF16), wmo_ref[...])
  o_ref[...] = x_ref[...] + _rms(y2, npost_ref[...])


def _mix_out(x2, og, ga, ybg, wgo, wmo, npost):
  t = x2.shape[0]
  row = pl.BlockSpec((OUT_ROWS, D_MODEL), lambda i: (i, 0))
  return pl.pallas_call(
      _mix_out_body,
      grid=(t // OUT_ROWS,),
      in_specs=[row, row, row, row, _resident((D_MODEL, D_MODEL)), _resident((D_MODEL, D_MODEL)),
                _resident((1, D_MODEL))],
      out_specs=row,
      out_shape=jax.ShapeDtypeStruct((t, D_MODEL), _F32),
      compiler_params=pltpu.CompilerParams(
          dimension_semantics=("arbitrary",), vmem_limit_bytes=VMEM_LIMIT_BYTES),
      name="mix_out",
  )(x2, og, ga, ybg, wgo, wmo, npost)


def _row(v):
  return v.reshape(1, -1).astype(_F32)


def _lane_tiles(w):
  k, c = w.shape
  return w.reshape(k, c // LANES, LANES).transpose(1, 0, 2).astype(_F32)


def kernel(x, ffn1_norm_pre, ffn1_norm_post, ffn1_w_in, ffn1_w_out, mix_norm_pre, mix_norm_post,
           mix_w_in, gdn_conv_w, gdn_a_log, gdn_dt_bias, gdn_norm_w, gdn_w_o, cnv_pw1_b, cnv_dw_w,
           cnv_dw_b, cnv_ln_g, cnv_ln_b, cnv_w_o, cnv_b_o, mix_w_out, ffn2_norm_pre,
           ffn2_norm_post, ffn2_w_in, ffn2_w_out):
  batch, seq, _ = x.shape
  depth = ffn1_w_in.shape[0]
  x2 = x.reshape(batch * seq, D_MODEL)
  o_qkv = 3 * HEADS * HEAD_DIM
  o_z = o_qkv + HEADS * HEAD_DIM
  o_b = o_z + HEADS
  o_a = o_b + HEADS
  o_glu = o_a + 2 * CNV_CH
  for i in range(depth):
    x2 = _ffn(x2, _row(ffn1_norm_pre[i]), _row(ffn1_norm_post[i]),
              ffn1_w_in[i].astype(_BF16), ffn1_w_out[i].astype(_BF16))

    w = mix_w_in[i]
    w_beta = w[:, o_z:o_b]
    w_a = w[:, o_b:o_a]
    wsm = jnp.concatenate(
        [w_beta, w_a, w_a, jnp.zeros((D_MODEL, SMALL_COLS - 3 * HEADS), _F32)], axis=1)
    wsmt = jnp.concatenate(
        [w_beta, w_a, w_a, jnp.zeros((D_MODEL, SMALL_ROWS - 3 * HEADS), _F32)], axis=1).T
    pad = jnp.zeros((HEADS,), _F32)
    a_log = jnp.concatenate([pad, gdn_a_log[i], gdn_a_log[i]])
    dt_b = jnp.concatenate([pad, gdn_dt_bias[i], gdn_dt_bias[i]])
    smc = jnp.zeros((SUBLANES, SMALL_COLS), _F32)
    smc = smc.at[0, :3 * HEADS].set(a_log).at[1, :3 * HEADS].set(dt_b)
    smct = jnp.zeros((SMALL_ROWS, LANES), _F32)
    smct = smct.at[:3 * HEADS, 0].set(a_log).at[:3 * HEADS, 1].set(dt_b)

    qkvh, gz, ga, ybg, bg, bgt = _mix_in(
        x2, _row(mix_norm_pre[i]), w[:, :o_qkv].astype(_BF16), w[:, o_qkv:o_z].astype(_BF16),
        w[:, o_a:o_glu].astype(_BF16), w[:, o_glu:].astype(_BF16), wsm.astype(_BF16),
        wsmt.astype(_BF16), _lane_tiles(gdn_conv_w[i]), smc, smct, _row(cnv_pw1_b[i]),
        _lane_tiles(cnv_dw_w[i]), _row(cnv_dw_b[i]), _row(cnv_ln_g[i]), _row(cnv_ln_b[i]),
        cnv_w_o[i].astype(_BF16), _row(cnv_b_o[i]), batch, seq)
    og = _gdn(qkvh, gz, bg, bgt, _row(gdn_norm_w[i]), batch, seq)
    x2 = _mix_out(x2, og, ga, ybg, gdn_w_o[i].astype(_BF16), mix_w_out[i].astype(_BF16),
                  _row(mix_norm_post[i]))

    x2 = _ffn(x2, _row(ffn2_norm_pre[i]), _row(ffn2_norm_post[i]),
              ffn2_w_in[i].astype(_BF16), ffn2_w_out[i].astype(_BF16))
  return x2.reshape(batch, seq, D_MODEL)
```

```python
import math

import jax
import jax.numpy as jnp
from jax import lax
from jax.experimental import pallas as pl
from jax.experimental.pallas import tpu as pltpu

D_MODEL = 1024
D_FF = 2816
HEADS = 8
HEAD_DIM = 128
GDN_CONV = 4
CNV_CH = 1024
CNV_K = 31
RMS_EPS = 1e-6
LN_EPS = 1e-5

LANES = 128
SUBLANES = 8
VMEM_LIMIT_BYTES = 56 * 1024 * 1024

FFN_ROWS = 512
FF_CHUNK = 1408
IN_ROWS = 512
MIX_ROWS = 256
PROJ_PIECE = 512
CONV_ROWS = 128
OUT_ROWS = 512
CHUNK = 64
QKV_HALO = SUBLANES
GLU_HALO = 4 * SUBLANES
SMALL_COLS = LANES
SMALL_ROWS = 4 * SUBLANES
CNV_TILES = CNV_CH // LANES

_BF16 = jnp.bfloat16
_F32 = jnp.float32


def _dot(a, b):
  return jnp.dot(a, b, preferred_element_type=_F32)


def _dot_nt(a, b):
  return lax.dot_general(a, b, (((1,), (1,)), ((), ())), preferred_element_type=_F32)


def _dot_tn(a, b):
  return lax.dot_general(a, b, (((0,), (0,)), ((), ())), preferred_element_type=_F32)


def _rms(x, w):
  ms = jnp.mean(x * x, axis=-1, keepdims=True)
  return x * lax.rsqrt(ms + RMS_EPS) * w


def _sigmoid(x):
  return 1.0 / (1.0 + jnp.exp(-x))


def _silu(x):
  return x * _sigmoid(x)


def _softplus(x):
  return jnp.maximum(x, 0.0) + jnp.log1p(jnp.exp(-jnp.abs(x)))


def _resident(shape):
  zeros = (0,) * len(shape)
  return pl.BlockSpec(shape, lambda *_: zeros, pipeline_mode=pl.Buffered(1))


def _vregs(x):
  return x.reshape(x.shape[0] // SUBLANES, SUBLANES, LANES)


def _ffn_body(x_ref, npre_ref, npost_ref, win_ref, wout_ref, o_ref):
  x = x_ref[...]
  h = _rms(x, npre_ref[...]).astype(_BF16)
  acc = jnp.zeros((FFN_ROWS, D_MODEL), _F32)
  for c in range(D_FF // FF_CHUNK):
    lo = c * FF_CHUNK
    gate = _dot(h, win_ref[:, lo:lo + FF_CHUNK])
    up = _dot(h, win_ref[:, D_FF + lo:D_FF + lo + FF_CHUNK])
    act = (_silu(gate) * up).astype(_BF16)
    acc = acc + _dot(act, wout_ref[lo:lo + FF_CHUNK, :])
  o_ref[...] = x + 0.5 * _rms(acc, npost_ref[...])


def _ffn(x2, npre, npost, w_in, w_out):
  t = x2.shape[0]
  row = pl.BlockSpec((FFN_ROWS, D_MODEL), lambda i: (i, 0))
  return pl.pallas_call(
      _ffn_body,
      grid=(t // FFN_ROWS,),
      in_specs=[row, _resident((1, D_MODEL)), _resident((1, D_MODEL)),
                _resident((D_MODEL, 2 * D_FF)), _resident((D_FF, D_MODEL))],
      out_specs=row,
      out_shape=jax.ShapeDtypeStruct((t, D_MODEL), _F32),
      compiler_params=pltpu.CompilerParams(
          dimension_semantics=("arbitrary",), vmem_limit_bytes=VMEM_LIMIT_BYTES),
      name="ffn",
  )(x2, npre, npost, w_in, w_out)


def _chunk_masks(n):
  shift = int(math.log2(CHUNK))
  row = lax.broadcasted_iota(jnp.int32, (n, n), 0)
  col = lax.broadcasted_iota(jnp.int32, (n, n), 1)
  same = lax.shift_right_logical(row, shift) == lax.shift_right_logical(col, shift)
  return same & (col <= row), same


def _mix_in_body(x_ref, npre_ref, wcat_ref, wglu_ref, wsm_ref, wsmt_ref,
                 smc_ref, smct_ref, pw1b_ref, dww_ref,
                 proj_ref, cv_ref, bg_ref, bgt_ref,
                 glu_ext, h_scr):
  n = IN_ROWS
  t = pl.program_id(1)

  @pl.when(t == 0)
  def _():
    glu_ext[:, 0:GLU_HALO, :] = jnp.zeros((CNV_TILES, GLU_HALO, LANES), _F32)

  h_scr[...] = _rms(x_ref[...], npre_ref[...]).astype(_BF16)
  h = h_scr[...]

  glu = _dot(h, wglu_ref[...]) + pw1b_ref[...]
  hh = glu[:, :CNV_CH] * _sigmoid(glu[:, CNV_CH:])
  for c in range(CNV_TILES):
    glu_ext[c, GLU_HALO:GLU_HALO + n, :] = hh[:, c * LANES:(c + 1) * LANES]

  tiles_per_piece = PROJ_PIECE // LANES
  reach = SUBLANES * ((CNV_K - 1) // SUBLANES)

  def conv_and_project(c, carry):
    p = _dot(h_scr[...], wcat_ref[c])
    for i in range(tiles_per_piece):
      proj_ref[c * tiles_per_piece + i] = p[:, i * LANES:(i + 1) * LANES]
    for r in range(0, n, CONV_ROWS):
      acc = jnp.zeros((CONV_ROWS // SUBLANES, SUBLANES, LANES), _F32)
      for s in range(SUBLANES):
        start = GLU_HALO - (CNV_K - 1) + s + r
        window = glu_ext[c, start:start + CONV_ROWS + reach, :]
        for j in range(s, CNV_K, SUBLANES):
          part = _vregs(window[j - s:j - s + CONV_ROWS, :])
          acc = acc + dww_ref[c, j][None] * part
      cv_ref[c, r:r + CONV_ROWS, :] = acc.reshape(CONV_ROWS, LANES)
    glu_ext[c, 0:GLU_HALO, :] = glu_ext[c, n:n + GLU_HALO, :]
    return carry

  lax.fori_loop(0, CNV_TILES, conv_and_project, 0)

  m = MIX_ROWS
  causal, same = _chunk_masks(m)
  upper = lax.broadcasted_iota(jnp.int32, (m, m), 0) <= lax.broadcasted_iota(jnp.int32, (m, m), 1)
  upper = (upper & same).astype(_BF16)
  lower_block = jnp.concatenate([causal.astype(_BF16), same.astype(_BF16)], axis=0)
  lane = lax.broadcasted_iota(jnp.int32, (m, SMALL_COLS), 1)
  sub = lax.broadcasted_iota(jnp.int32, (SMALL_ROWS, m), 0)

  def split3(v, axis):
    hi = v.astype(_BF16)
    rest = v - hi.astype(_F32)
    mid = rest.astype(_BF16)
    lo = (rest - mid.astype(_F32)).astype(_BF16)
    return jnp.concatenate([hi, mid, lo], axis=axis)

  for r in range(0, n, m):
    hr = h_scr[r:r + m, :]
    sm = _dot(hr, wsm_ref[...])
    g = -jnp.exp(smc_ref[0:1, :]) * _softplus(sm + smc_ref[1:2, :])
    sums = _dot(lower_block, split3(g, 1))
    sums = sums[:, :SMALL_COLS] + sums[:, SMALL_COLS:2 * SMALL_COLS] + sums[:, 2 * SMALL_COLS:]
    bg_ref[r:r + m, :] = jnp.where(lane < HEADS, _sigmoid(sm),
                                   jnp.where(lane < 2 * HEADS, sums[:m], sums[m:]))
    smt = _dot_nt(wsmt_ref[...], hr)
    gt = -jnp.exp(smct_ref[:, 0:1]) * _softplus(smt + smct_ref[:, 1:2])
    sums_t = _dot(split3(gt, 0), upper)
    sums_t = sums_t[:SMALL_ROWS] + sums_t[SMALL_ROWS:2 * SMALL_ROWS] + sums_t[2 * SMALL_ROWS:]
    bgt_ref[:, r:r + m] = jnp.where(sub < HEADS, _sigmoid(smt), sums_t)


def _mix_in(x2, npre, wcat, wglu, wsm, wsmt, smc, smct, pw1b, dww, batch, seq):
  t = x2.shape[0]
  nt = seq // IN_ROWS
  row = lambda width: pl.BlockSpec((IN_ROWS, width), lambda b, i: (b * nt + i, 0))
  tiles = lambda count: pl.BlockSpec((count, IN_ROWS, LANES), lambda b, i: (0, b * nt + i, 0))
  weights = [npre, wcat, wglu, wsm, wsmt, smc, smct, pw1b, dww]
  out_shape = (
      jax.ShapeDtypeStruct((4 * HEADS, t, LANES), _F32),
      jax.ShapeDtypeStruct((CNV_TILES, t, LANES), _F32),
      jax.ShapeDtypeStruct((t, SMALL_COLS), _F32),
      jax.ShapeDtypeStruct((SMALL_ROWS, t), _F32),
  )
  out_specs = (
      tiles(4 * HEADS), tiles(CNV_TILES), row(SMALL_COLS),
      pl.BlockSpec((SMALL_ROWS, IN_ROWS), lambda b, i: (0, b * nt + i)),
  )
  return pl.pallas_call(
      _mix_in_body,
      grid=(batch, nt),
      in_specs=[row(D_MODEL)] + [_resident(w.shape) for w in weights],
      out_specs=out_specs,
      out_shape=out_shape,
      scratch_shapes=[
          pltpu.VMEM((CNV_TILES, GLU_HALO + IN_ROWS, LANES), _F32),
          pltpu.VMEM((IN_ROWS, D_MODEL), _BF16),
      ],
      compiler_params=pltpu.CompilerParams(
          dimension_semantics=("arbitrary", "arbitrary"), vmem_limit_bytes=VMEM_LIMIT_BYTES),
      name="mix_in",
  )(x2, *weights)


def _gdn_body(q_ref, k_ref, v_ref, z_ref, bg_ref, bgt_ref, nw_ref, convw_ref, o_ref, state, ext):
  n = MIX_ROWS
  t = pl.program_id(1)
  heads = range(HEADS)

  @pl.when(t == 0)
  def _():
    state[...] = jnp.zeros((HEADS, HEAD_DIM, HEAD_DIM), _F32)
    ext[:, 0:QKV_HALO, :] = jnp.zeros((3 * HEADS, QKV_HALO, LANES), _F32)

  def conv(part, ref, j):
    c = part * HEADS + j
    ext[c, QKV_HALO:QKV_HALO + n, :] = ref[j]
    acc = jnp.zeros((n // SUBLANES, SUBLANES, LANES), _F32)
    for tap in range(GDN_CONV):
      start = QKV_HALO - (GDN_CONV - 1) + tap
      acc = acc + convw_ref[c, tap][None] * _vregs(ext[c, start:start + n, :])
    ext[c, 0:QKV_HALO, :] = ext[c, n:n + QKV_HALO, :]
    y = _silu(acc.reshape(n, LANES))
    if part < 2:
      y = y * lax.rsqrt(jnp.sum(y * y, axis=-1, keepdims=True) + 1e-6)
    if part == 0:
      y = y * (HEAD_DIM ** -0.5)
    return y

  bg = bg_ref[...]
  causal, _ = _chunk_masks(n)
  row = lax.broadcasted_iota(jnp.int32, (n, n), 0)
  col = lax.broadcasted_iota(jnp.int32, (n, n), 1)
  strict = causal & (col < row)
  eye = (row == col).astype(_F32)

  beta = [bg[:, j:j + 1] for j in heads]
  g_col = [bg[:, HEADS + j:HEADS + j + 1] for j in heads]
  g_last = [bg[:, 2 * HEADS + j:2 * HEADS + j + 1] for j in heads]
  qs, ks, vs, qkb, xs, ps = [], [], [], [], [], []
  for j in heads:
    q = conv(0, q_ref, j)
    k = conv(1, k_ref, j)
    v = conv(2, v_ref, j)
    g_row = bgt_ref[HEADS + j:HEADS + j + 1, :]
    d = jnp.where(causal, jnp.exp(jnp.where(causal, g_col[j] - g_row, 0.0)), 0.0)
    kb = k.astype(_BF16)
    qk = (_dot_nt(q.astype(_BF16), kb) * d).astype(_BF16)
    qkb.append([qk[c * CHUNK:(c + 1) * CHUNK, c * CHUNK:(c + 1) * CHUNK]
                for c in range(n // CHUNK)])
    a_mat = jnp.where(strict, _dot_nt(kb, kb) * d * beta[j], 0.0)
    xs.append(eye - a_mat)
    ps.append(a_mat)
    qs.append(q)
    ks.append(k)
    vs.append(v)

  for _ in range(int(math.log2(CHUNK)) - 1):
    for j in heads:
      pb = ps[j].astype(_BF16)
      ps[j] = _dot(pb, pb)
      xs[j] = xs[j] + _dot(xs[j].astype(_BF16), ps[j].astype(_BF16))

  us, ws, q_dec, k_dec = [], [], [], []
  for j in heads:
    e_g = jnp.exp(g_col[j])
    rhs = jnp.concatenate([vs[j] * beta[j], ks[j] * (beta[j] * e_g)], axis=-1).astype(_BF16)
    uw = _dot(xs[j].astype(_BF16), rhs)
    us.append(uw[:, :HEAD_DIM])
    ws.append(uw[:, HEAD_DIM:])
    q_dec.append(qs[j] * e_g)
    k_dec.append((ks[j] * jnp.exp(g_last[j] - g_col[j])).astype(_BF16))

  s = [state[j] for j in heads]
  for c in range(n // CHUNK):
    r = slice(c * CHUNK, (c + 1) * CHUNK)
    ws_qs = []
    for j in heads:
      wq = jnp.concatenate([ws[j][r], q_dec[j][r]], axis=0).astype(_BF16)
      ws_qs.append(_dot(wq, s[j].astype(_BF16)))
    for j in heads:
      v_new = (us[j][r] - ws_qs[j][:CHUNK]).astype(_BF16)
      o_c = ws_qs[j][CHUNK:] + _dot(qkb[j][c], v_new)
      d_c = jnp.exp(g_last[j][c * CHUNK:c * CHUNK + 1, :])
      s[j] = s[j] * d_c + _dot_tn(k_dec[j][r], v_new)
      o_n = _rms(o_c, nw_ref[...]) * _silu(z_ref[j, r, :])
      o_ref[r, j * HEAD_DIM:(j + 1) * HEAD_DIM] = o_n.astype(o_ref.dtype)
  for j in heads:
    state[j] = s[j]


def _gdn(proj, bg, bgt, norm_w, convw, batch, seq):
  t = bg.shape[0]
  nt = seq // MIX_ROWS

  def head_major(part):
    return pl.BlockSpec((HEADS, MIX_ROWS, LANES), lambda b, i: (part, b * nt + i, 0))

  return pl.pallas_call(
      _gdn_body,
      grid=(batch, nt),
      in_specs=[head_major(0), head_major(1), head_major(2), head_major(3),
                pl.BlockSpec((MIX_ROWS, SMALL_COLS), lambda b, i: (b * nt + i, 0)),
                pl.BlockSpec((SMALL_ROWS, MIX_ROWS), lambda b, i: (0, b * nt + i)),
                _resident((1, HEAD_DIM)), _resident(convw.shape)],
      out_specs=pl.BlockSpec((MIX_ROWS, D_MODEL), lambda b, i: (b * nt + i, 0)),
      out_shape=jax.ShapeDtypeStruct((t, D_MODEL), _BF16),
      scratch_shapes=[pltpu.VMEM((HEADS, HEAD_DIM, HEAD_DIM), _F32),
                      pltpu.VMEM((3 * HEADS, QKV_HALO + MIX_ROWS, LANES), _F32)],
      compiler_params=pltpu.CompilerParams(
          dimension_semantics=("arbitrary", "arbitrary"), vmem_limit_bytes=VMEM_LIMIT_BYTES),
      name="gdn",
  )(proj, proj, proj, proj, bg, bgt, norm_w, convw)


def _mix_out_body(x_ref, og_ref, cv_ref, npre_ref, wgat_ref, wgo_ref, wmo_ref, cwo_ref, dwb_ref,
                  lng_ref, lnb_ref, cbo_ref, npost_ref, o_ref):
  x = x_ref[...]
  gates = _sigmoid(_dot(_rms(x, npre_ref[...]).astype(_BF16), wgat_ref[...]))
  cv = jnp.concatenate([cv_ref[c] for c in range(CNV_TILES)], axis=-1) + dwb_ref[...]
  mu = jnp.mean(cv, axis=-1, keepdims=True)
  cen = cv - mu
  var = jnp.mean(cen * cen, axis=-1, keepdims=True)
  ln = cen * lax.rsqrt(var + LN_EPS) * lng_ref[...] + lnb_ref[...]
  y_b = _dot(_silu(ln).astype(_BF16), cwo_ref[...]) + cbo_ref[...]
  y_a = _dot(og_ref[...], wgo_ref[...])
  y = gates[:, :D_MODEL] * y_a + gates[:, D_MODEL:] * y_b
  y2 = _dot(y.astype(_BF16), wmo_ref[...])
  o_ref[...] = x + _rms(y2, npost_ref[...])


def _mix_out(x2, og, cv, npre, wgat, wgo, wmo, cwo, dwb, lng, lnb, cbo, npost):
  t = x2.shape[0]
  row = pl.BlockSpec((OUT_ROWS, D_MODEL), lambda i: (i, 0))
  square = _resident((D_MODEL, D_MODEL))
  vec = _resident((1, D_MODEL))
  return pl.pallas_call(
      _mix_out_body,
      grid=(t // OUT_ROWS,),
      in_specs=[row, row, pl.BlockSpec((CNV_TILES, OUT_ROWS, LANES), lambda i: (0, i, 0)),
                vec, _resident((D_MODEL, 2 * D_MODEL)), square, square, square,
                vec, vec, vec, vec, vec],
      out_specs=row,
      out_shape=jax.ShapeDtypeStruct((t, D_MODEL), _F32),
      compiler_params=pltpu.CompilerParams(
          dimension_semantics=("arbitrary",), vmem_limit_bytes=VMEM_LIMIT_BYTES),
      name="mix_out",
  )(x2, og, cv, npre, wgat, wgo, wmo, cwo, dwb, lng, lnb, cbo, npost)


def _row(v):
  return v.reshape(1, -1).astype(_F32)


def _column_pieces(w):
  k, c = w.shape
  return w.reshape(k, c // PROJ_PIECE, PROJ_PIECE).transpose(1, 0, 2)


def _lane_tiles(w):
  k, c = w.shape
  tiles = w.reshape(k, c // LANES, LANES).transpose(1, 0, 2).astype(_F32)
  return jnp.broadcast_to(tiles[:, :, None, :], (c // LANES, k, SUBLANES, LANES))


def kernel(x, ffn1_norm_pre, ffn1_norm_post, ffn1_w_in, ffn1_w_out, mix_norm_pre, mix_norm_post,
           mix_w_in, gdn_conv_w, gdn_a_log, gdn_dt_bias, gdn_norm_w, gdn_w_o, cnv_pw1_b, cnv_dw_w,
           cnv_dw_b, cnv_ln_g, cnv_ln_b, cnv_w_o, cnv_b_o, mix_w_out, ffn2_norm_pre,
           ffn2_norm_post, ffn2_w_in, ffn2_w_out):
  batch, seq, _ = x.shape
  depth = ffn1_w_in.shape[0]
  x2 = x.reshape(batch * seq, D_MODEL)
  o_qkv = 3 * HEADS * HEAD_DIM
  o_z = o_qkv + HEADS * HEAD_DIM
  o_b = o_z + HEADS
  o_a = o_b + HEADS
  o_glu = o_a + 2 * CNV_CH
  for i in range(depth):
    x2 = _ffn(x2, _row(ffn1_norm_pre[i]), _row(ffn1_norm_post[i]),
              ffn1_w_in[i].astype(_BF16), ffn1_w_out[i].astype(_BF16))

    w = mix_w_in[i]
    w_beta = w[:, o_z:o_b]
    w_a = w[:, o_b:o_a]
    wsm = jnp.concatenate(
        [w_beta, w_a, w_a, jnp.zeros((D_MODEL, SMALL_COLS - 3 * HEADS), _F32)], axis=1)
    wsmt = jnp.concatenate(
        [w_beta, w_a, w_a, jnp.zeros((D_MODEL, SMALL_ROWS - 3 * HEADS), _F32)], axis=1).T
    pad = jnp.zeros((HEADS,), _F32)
    a_log = jnp.concatenate([pad, gdn_a_log[i], gdn_a_log[i]])
    dt_b = jnp.concatenate([pad, gdn_dt_bias[i], gdn_dt_bias[i]])
    smc = jnp.zeros((SUBLANES, SMALL_COLS), _F32)
    smc = smc.at[0, :3 * HEADS].set(a_log).at[1, :3 * HEADS].set(dt_b)
    smct = jnp.zeros((SMALL_ROWS, LANES), _F32)
    smct = smct.at[:3 * HEADS, 0].set(a_log).at[:3 * HEADS, 1].set(dt_b)

    npre = _row(mix_norm_pre[i])
    proj, cv, bg, bgt = _mix_in(
        x2, npre, _column_pieces(w[:, :o_z].astype(_BF16)), w[:, o_a:o_glu].astype(_BF16),
        wsm.astype(_BF16), wsmt.astype(_BF16), smc, smct, _row(cnv_pw1_b[i]),
        _lane_tiles(cnv_dw_w[i]), batch, seq)
    og = _gdn(proj, bg, bgt, _row(gdn_norm_w[i]), _lane_tiles(gdn_conv_w[i]), batch, seq)
    x2 = _mix_out(x2, og, cv, npre, w[:, o_glu:].astype(_BF16), gdn_w_o[i].astype(_BF16),
                  mix_w_out[i].astype(_BF16), cnv_w_o[i].astype(_BF16), _row(cnv_dw_b[i]),
                  _row(cnv_ln_g[i]), _row(cnv_ln_b[i]), _row(cnv_b_o[i]), _row(mix_norm_post[i]))

    x2 = _ffn(x2, _row(ffn2_norm_pre[i]), _row(ffn2_norm_post[i]),
              ffn2_w_in[i].astype(_BF16), ffn2_w_out[i].astype(_BF16))
  return x2.reshape(batch, seq, D_MODEL)
```

```python
import math

import jax
import jax.numpy as jnp
from jax import lax
from jax.experimental import pallas as pl
from jax.experimental.pallas import tpu as pltpu

D_MODEL = 1024
D_FF = 2816
HEADS = 8
HEAD_DIM = 128
GDN_CONV = 4
CNV_CH = 1024
CNV_K = 31
RMS_EPS = 1e-6
LN_EPS = 1e-5

LANES = 128
SUBLANES = 8
VMEM_LIMIT_BYTES = 56 * 1024 * 1024

FFN_ROWS = 512
FF_CHUNK = 2816
IN_ROWS = 256
MIX_ROWS = 256
PROJ_PIECE = 512
CONV_ROWS = 128
OUT_ROWS = 512
CAST_BYTES = 4 * 1024 * 1024
CHUNK = 64
QKV_HALO = SUBLANES
GLU_HALO = 4 * SUBLANES
SMALL_COLS = LANES
SMALL_ROWS = 4 * SUBLANES
CNV_TILES = CNV_CH // LANES

_BF16 = jnp.bfloat16
_F32 = jnp.float32


def _dot(a, b):
  return jnp.dot(a, b, preferred_element_type=_F32)


def _dot_nt(a, b):
  return lax.dot_general(a, b, (((1,), (1,)), ((), ())), preferred_element_type=_F32)


def _dot_tn(a, b):
  return lax.dot_general(a, b, (((0,), (0,)), ((), ())), preferred_element_type=_F32)


def _rms(x, w):
  ms = jnp.mean(x * x, axis=-1, keepdims=True)
  return x * lax.rsqrt(ms + RMS_EPS) * w


def _sigmoid(x):
  return 1.0 / (1.0 + jnp.exp(-x))


def _silu(x):
  return x * _sigmoid(x)


def _softplus(x):
  return jnp.maximum(x, 0.0) + jnp.log1p(jnp.exp(-jnp.abs(x)))


def _resident(shape):
  zeros = (0,) * len(shape)
  return pl.BlockSpec(shape, lambda *_: zeros, pipeline_mode=pl.Buffered(1))


def _vregs(x):
  return x.reshape(x.shape[0] // SUBLANES, SUBLANES, LANES)


def _ffn_body(x_ref, npre_ref, npost_ref, win_ref, wout_ref, o_ref):
  x = x_ref[...]
  h = _rms(x, npre_ref[...]).astype(_BF16)
  acc = jnp.zeros((FFN_ROWS, D_MODEL), _F32)
  for c in range(D_FF // FF_CHUNK):
    lo = c * FF_CHUNK
    gate = _dot(h, win_ref[:, lo:lo + FF_CHUNK])
    up = _dot(h, win_ref[:, D_FF + lo:D_FF + lo + FF_CHUNK])
    act = (_silu(gate) * up).astype(_BF16)
    acc = acc + _dot(act, wout_ref[lo:lo + FF_CHUNK, :])
  o_ref[...] = x + 0.5 * _rms(acc, npost_ref[...])


def _ffn(x2, npre, npost, w_in, w_out):
  t = x2.shape[0]
  row = pl.BlockSpec((FFN_ROWS, D_MODEL), lambda i: (i, 0))
  return pl.pallas_call(
      _ffn_body,
      grid=(t // FFN_ROWS,),
      in_specs=[row, _resident((1, D_MODEL)), _resident((1, D_MODEL)),
                _resident((D_MODEL, 2 * D_FF)), _resident((D_FF, D_MODEL))],
      out_specs=row,
      out_shape=jax.ShapeDtypeStruct((t, D_MODEL), _F32),
      compiler_params=pltpu.CompilerParams(
          dimension_semantics=("arbitrary",), vmem_limit_bytes=VMEM_LIMIT_BYTES),
      name="ffn",
  )(x2, npre, npost, w_in, w_out)


def _chunk_masks(n):
  shift = int(math.log2(CHUNK))
  row = lax.broadcasted_iota(jnp.int32, (n, n), 0)
  col = lax.broadcasted_iota(jnp.int32, (n, n), 1)
  same = lax.shift_right_logical(row, shift) == lax.shift_right_logical(col, shift)
  return same & (col <= row), same


def _mix_in_body(x_ref, npre_ref, wcat_ref, wglu_ref, wsm_ref, wsmt_ref,
                 smc_ref, smct_ref, pw1b_ref, dww_ref,
                 proj_ref, cv_ref, bg_ref, bgt_ref,
                 glu_ext, h_scr):
  n = IN_ROWS
  t = pl.program_id(1)

  @pl.when(t == 0)
  def _():
    glu_ext[:, 0:GLU_HALO, :] = jnp.zeros((CNV_TILES, GLU_HALO, LANES), _F32)

  h_scr[...] = _rms(x_ref[...], npre_ref[...]).astype(_BF16)
  h = h_scr[...]

  glu = _dot(h, wglu_ref[...]) + pw1b_ref[...]
  hh = glu[:, :CNV_CH] * _sigmoid(glu[:, CNV_CH:])
  for c in range(CNV_TILES):
    glu_ext[c, GLU_HALO:GLU_HALO + n, :] = hh[:, c * LANES:(c + 1) * LANES]

  tiles_per_piece = PROJ_PIECE // LANES
  reach = SUBLANES * ((CNV_K - 1) // SUBLANES)

  def conv_and_project(c, carry):
    for r in range(0, n, CONV_ROWS):
      acc = jnp.zeros((CONV_ROWS // SUBLANES, SUBLANES, LANES), _F32)
      for s in range(SUBLANES):
        start = GLU_HALO - (CNV_K - 1) + s + r
        window = glu_ext[c, start:start + CONV_ROWS + reach, :]
        for j in range(s, CNV_K, SUBLANES):
          part = _vregs(window[j - s:j - s + CONV_ROWS, :])
          acc = acc + dww_ref[c, j][None] * part
      cv_ref[c, r:r + CONV_ROWS, :] = acc.reshape(CONV_ROWS, LANES)
    glu_ext[c, 0:GLU_HALO, :] = glu_ext[c, n:n + GLU_HALO, :]
    p = _dot(h_scr[...], wcat_ref[c])
    for i in range(tiles_per_piece):
      proj_ref[c * tiles_per_piece + i] = p[:, i * LANES:(i + 1) * LANES]
    return carry

  lax.fori_loop(0, CNV_TILES, conv_and_project, 0)

  m = MIX_ROWS
  causal, same = _chunk_masks(m)
  upper = lax.broadcasted_iota(jnp.int32, (m, m), 0) <= lax.broadcasted_iota(jnp.int32, (m, m), 1)
  upper = (upper & same).astype(_BF16)
  lower_block = jnp.concatenate([causal.astype(_BF16), same.astype(_BF16)], axis=0)
  lane = lax.broadcasted_iota(jnp.int32, (m, SMALL_COLS), 1)
  sub = lax.broadcasted_iota(jnp.int32, (SMALL_ROWS, m), 0)

  def split3(v, axis):
    hi = v.astype(_BF16)
    rest = v - hi.astype(_F32)
    mid = rest.astype(_BF16)
    lo = (rest - mid.astype(_F32)).astype(_BF16)
    return jnp.concatenate([hi, mid, lo], axis=axis)

  for r in range(0, n, m):
    hr = h_scr[r:r + m, :]
    sm = _dot(hr, wsm_ref[...])
    g = -jnp.exp(smc_ref[0:1, :]) * _softplus(sm + smc_ref[1:2, :])
    sums = _dot(lower_block, split3(g, 1))
    sums = sums[:, :SMALL_COLS] + sums[:, SMALL_COLS:2 * SMALL_COLS] + sums[:, 2 * SMALL_COLS:]
    bg_ref[r:r + m, :] = jnp.where(lane < HEADS, _sigmoid(sm),
                                   jnp.where(lane < 2 * HEADS, sums[:m], sums[m:]))
    smt = _dot_nt(wsmt_ref[...], hr)
    gt = -jnp.exp(smct_ref[:, 0:1]) * _softplus(smt + smct_ref[:, 1:2])
    sums_t = _dot(split3(gt, 0), upper)
    sums_t = sums_t[:SMALL_ROWS] + sums_t[SMALL_ROWS:2 * SMALL_ROWS] + sums_t[2 * SMALL_ROWS:]
    bgt_ref[:, r:r + m] = jnp.where(sub < HEADS, _sigmoid(smt), sums_t)


def _mix_in(x2, npre, wcat, wglu, wsm, wsmt, smc, smct, pw1b, dww, batch, seq):
  t = x2.shape[0]
  nt = seq // IN_ROWS
  row = lambda width: pl.BlockSpec((IN_ROWS, width), lambda b, i: (b * nt + i, 0))
  tiles = lambda count: pl.BlockSpec((count, IN_ROWS, LANES), lambda b, i: (0, b * nt + i, 0))
  weights = [npre, wcat, wglu, wsm, wsmt, smc, smct, pw1b, dww]
  out_shape = (
      jax.ShapeDtypeStruct((4 * HEADS, t, LANES), _F32),
      jax.ShapeDtypeStruct((CNV_TILES, t, LANES), _F32),
      jax.ShapeDtypeStruct((t, SMALL_COLS), _F32),
      jax.ShapeDtypeStruct((SMALL_ROWS, t), _F32),
  )
  out_specs = (
      tiles(4 * HEADS), tiles(CNV_TILES), row(SMALL_COLS),
      pl.BlockSpec((SMALL_ROWS, IN_ROWS), lambda b, i: (0, b * nt + i)),
  )
  return pl.pallas_call(
      _mix_in_body,
      grid=(batch, nt),
      in_specs=[row(D_MODEL)] + [_resident(w.shape) for w in weights],
      out_specs=out_specs,
      out_shape=out_shape,
      scratch_shapes=[
          pltpu.VMEM((CNV_TILES, GLU_HALO + IN_ROWS, LANES), _F32),
          pltpu.VMEM((IN_ROWS, D_MODEL), _BF16),
      ],
      compiler_params=pltpu.CompilerParams(
          dimension_semantics=("arbitrary", "arbitrary"), vmem_limit_bytes=VMEM_LIMIT_BYTES),
      name="mix_in",
  )(x2, *weights)


def _gdn_body(q_ref, k_ref, v_ref, z_ref, bg_ref, bgt_ref, nw_ref, convw_ref, o_ref, state, ext):
  n = MIX_ROWS
  t = pl.program_id(1)
  heads = range(HEADS)

  @pl.when(t == 0)
  def _():
    state[...] = jnp.zeros((HEADS, HEAD_DIM, HEAD_DIM), _F32)
    ext[:, 0:QKV_HALO, :] = jnp.zeros((3 * HEADS, QKV_HALO, LANES), _F32)

  def conv(part, ref, j):
    c = part * HEADS + j
    ext[c, QKV_HALO:QKV_HALO + n, :] = ref[j]
    acc = jnp.zeros((n // SUBLANES, SUBLANES, LANES), _F32)
    for tap in range(GDN_CONV):
      start = QKV_HALO - (GDN_CONV - 1) + tap
      acc = acc + convw_ref[c, tap][None] * _vregs(ext[c, start:start + n, :])
    ext[c, 0:QKV_HALO, :] = ext[c, n:n + QKV_HALO, :]
    y = _silu(acc.reshape(n, LANES))
    if part < 2:
      y = y * lax.rsqrt(jnp.sum(y * y, axis=-1, keepdims=True) + 1e-6)
    if part == 0:
      y = y * (HEAD_DIM ** -0.5)
    return y

  bg = bg_ref[...]
  causal, _ = _chunk_masks(n)
  row = lax.broadcasted_iota(jnp.int32, (n, n), 0)
  col = lax.broadcasted_iota(jnp.int32, (n, n), 1)
  strict = causal & (col < row)
  eye = (row == col).astype(_F32)

  beta = [bg[:, j:j + 1] for j in heads]
  g_col = [bg[:, HEADS + j:HEADS + j + 1] for j in heads]
  g_last = [bg[:, 2 * HEADS + j:2 * HEADS + j + 1] for j in heads]
  qs, ks, vs, qkb, xs, ps = [], [], [], [], [], []
  for j in heads:
    q = conv(0, q_ref, j)
    k = conv(1, k_ref, j)
    v = conv(2, v_ref, j)
    g_row = bgt_ref[HEADS + j:HEADS + j + 1, :]
    d = jnp.where(causal, jnp.exp(jnp.where(causal, g_col[j] - g_row, 0.0)), 0.0)
    kb = k.astype(_BF16)
    qk = (_dot_nt(q.astype(_BF16), kb) * d).astype(_BF16)
    qkb.append([qk[c * CHUNK:(c + 1) * CHUNK, c * CHUNK:(c + 1) * CHUNK]
                for c in range(n // CHUNK)])
    a_mat = jnp.where(strict, _dot_nt(kb, kb) * d * beta[j], 0.0)
    xs.append(eye - a_mat)
    ps.append(a_mat)
    qs.append(q)
    ks.append(k)
    vs.append(v)

  for _ in range(int(math.log2(CHUNK)) - 1):
    for j in heads:
      pb = ps[j].astype(_BF16)
      ps[j] = _dot(pb, pb)
      xs[j] = xs[j] + _dot(xs[j].astype(_BF16), ps[j].astype(_BF16))

  us, ws, q_dec, k_dec = [], [], [], []
  for j in heads:
    e_g = jnp.exp(g_col[j])
    rhs = jnp.concatenate([vs[j] * beta[j], ks[j] * (beta[j] * e_g)], axis=-1).astype(_BF16)
    uw = _dot(xs[j].astype(_BF16), rhs)
    us.append(uw[:, :HEAD_DIM])
    ws.append(uw[:, HEAD_DIM:])
    q_dec.append(qs[j] * e_g)
    k_dec.append((ks[j] * jnp.exp(g_last[j] - g_col[j])).astype(_BF16))

  s = [state[j] for j in heads]
  for c in range(n // CHUNK):
    r = slice(c * CHUNK, (c + 1) * CHUNK)
    ws_qs = []
    for j in heads:
      wq = jnp.concatenate([ws[j][r], q_dec[j][r]], axis=0).astype(_BF16)
      ws_qs.append(_dot(wq, s[j].astype(_BF16)))
    for j in heads:
      v_new = (us[j][r] - ws_qs[j][:CHUNK]).astype(_BF16)
      o_c = ws_qs[j][CHUNK:] + _dot(qkb[j][c], v_new)
      d_c = jnp.exp(g_last[j][c * CHUNK:c * CHUNK + 1, :])
      s[j] = s[j] * d_c + _dot_tn(k_dec[j][r], v_new)
      o_n = _rms(o_c, nw_ref[...]) * _silu(z_ref[j, r, :])
      o_ref[r, j * HEAD_DIM:(j + 1) * HEAD_DIM] = o_n.astype(o_ref.dtype)
  for j in heads:
    state[j] = s[j]


def _gdn(proj, bg, bgt, norm_w, convw, batch, seq):
  t = bg.shape[0]
  nt = seq // MIX_ROWS

  def head_major(part):
    return pl.BlockSpec((HEADS, MIX_ROWS, LANES), lambda b, i: (part, b * nt + i, 0))

  return pl.pallas_call(
      _gdn_body,
      grid=(batch, nt),
      in_specs=[head_major(0), head_major(1), head_major(2), head_major(3),
                pl.BlockSpec((MIX_ROWS, SMALL_COLS), lambda b, i: (b * nt + i, 0)),
                pl.BlockSpec((SMALL_ROWS, MIX_ROWS), lambda b, i: (0, b * nt + i)),
                _resident((1, HEAD_DIM)), _resident(convw.shape)],
      out_specs=pl.BlockSpec((MIX_ROWS, D_MODEL), lambda b, i: (b * nt + i, 0)),
      out_shape=jax.ShapeDtypeStruct((t, D_MODEL), _BF16),
      scratch_shapes=[pltpu.VMEM((HEADS, HEAD_DIM, HEAD_DIM), _F32),
                      pltpu.VMEM((3 * HEADS, QKV_HALO + MIX_ROWS, LANES), _F32)],
      compiler_params=pltpu.CompilerParams(
          dimension_semantics=("arbitrary", "arbitrary"), vmem_limit_bytes=VMEM_LIMIT_BYTES),
      name="gdn",
  )(proj, proj, proj, proj, bg, bgt, norm_w, convw)


def _mix_out_body(x_ref, og_ref, cv_ref, npre_ref, wgat_ref, wgo_ref, wmo_ref, cwo_ref, dwb_ref,
                  lng_ref, lnb_ref, cbo_ref, npost_ref, o_ref):
  x = x_ref[...]
  gates = _sigmoid(_dot(_rms(x, npre_ref[...]).astype(_BF16), wgat_ref[...]))
  cv = jnp.concatenate([cv_ref[c] for c in range(CNV_TILES)], axis=-1) + dwb_ref[...]
  mu = jnp.mean(cv, axis=-1, keepdims=True)
  cen = cv - mu
  var = jnp.mean(cen * cen, axis=-1, keepdims=True)
  ln = cen * lax.rsqrt(var + LN_EPS) * lng_ref[...] + lnb_ref[...]
  y_b = _dot(_silu(ln).astype(_BF16), cwo_ref[...]) + cbo_ref[...]
  y_a = _dot(og_ref[...], wgo_ref[...])
  y = gates[:, :D_MODEL] * y_a + gates[:, D_MODEL:] * y_b
  y2 = _dot(y.astype(_BF16), wmo_ref[...])
  o_ref[...] = x + _rms(y2, npost_ref[...])


def _mix_out(x2, og, cv, npre, wgat, wgo, wmo, cwo, dwb, lng, lnb, cbo, npost):
  t = x2.shape[0]
  row = pl.BlockSpec((OUT_ROWS, D_MODEL), lambda i: (i, 0))
  square = _resident((D_MODEL, D_MODEL))
  vec = _resident((1, D_MODEL))
  return pl.pallas_call(
      _mix_out_body,
      grid=(t // OUT_ROWS,),
      in_specs=[row, row, pl.BlockSpec((CNV_TILES, OUT_ROWS, LANES), lambda i: (0, i, 0)),
                vec, _resident((D_MODEL, 2 * D_MODEL)), square, square, square,
                vec, vec, vec, vec, vec],
      out_specs=row,
      out_shape=jax.ShapeDtypeStruct((t, D_MODEL), _F32),
      compiler_params=pltpu.CompilerParams(
          dimension_semantics=("arbitrary",), vmem_limit_bytes=VMEM_LIMIT_BYTES),
      name="mix_out",
  )(x2, og, cv, npre, wgat, wgo, wmo, cwo, dwb, lng, lnb, cbo, npost)


def _cast_body(w_ref, o_ref):
  o_ref[...] = w_ref[...].astype(_BF16)


def _to_bf16(w):
  rows, cols = w.shape
  block = rows
  while block * cols * 4 > CAST_BYTES and block % 32 == 0:
    block //= 2
  spec = pl.BlockSpec((block, cols), lambda i: (i, 0))
  return pl.pallas_call(
      _cast_body,
      grid=(rows // block,),
      in_specs=[spec],
      out_specs=spec,
      out_shape=jax.ShapeDtypeStruct((rows, cols), _BF16),
      compiler_params=pltpu.CompilerParams(
          dimension_semantics=("arbitrary",), vmem_limit_bytes=VMEM_LIMIT_BYTES),
      name="cast",
  )(w)


def _row(v):
  return v.reshape(1, -1).astype(_F32)


def _column_pieces(w):
  k, c = w.shape
  return w.reshape(k, c // PROJ_PIECE, PROJ_PIECE).transpose(1, 0, 2)


def _lane_tiles(w):
  k, c = w.shape
  tiles = w.reshape(k, c // LANES, LANES).transpose(1, 0, 2).astype(_F32)
  return jnp.broadcast_to(tiles[:, :, None, :], (c // LANES, k, SUBLANES, LANES))


def kernel(x, ffn1_norm_pre, ffn1_norm_post, ffn1_w_in, ffn1_w_out, mix_norm_pre, mix_norm_post,
           mix_w_in, gdn_conv_w, gdn_a_log, gdn_dt_bias, gdn_norm_w, gdn_w_o, cnv_pw1_b, cnv_dw_w,
           cnv_dw_b, cnv_ln_g, cnv_ln_b, cnv_w_o, cnv_b_o, mix_w_out, ffn2_norm_pre,
           ffn2_norm_post, ffn2_w_in, ffn2_w_out):
  batch, seq, _ = x.shape
  depth = ffn1_w_in.shape[0]
  x2 = x.reshape(batch * seq, D_MODEL)
  o_qkv = 3 * HEADS * HEAD_DIM
  o_z = o_qkv + HEADS * HEAD_DIM
  o_b = o_z + HEADS
  o_a = o_b + HEADS
  o_glu = o_a + 2 * CNV_CH
  for i in range(depth):
    x2 = _ffn(x2, _row(ffn1_norm_pre[i]), _row(ffn1_norm_post[i]),
              _to_bf16(ffn1_w_in[i]), _to_bf16(ffn1_w_out[i]))

    wb = _to_bf16(mix_w_in[i])
    w_beta = wb[:, o_z:o_b]
    w_a = wb[:, o_b:o_a]
    wsm = jnp.concatenate(
        [w_beta, w_a, w_a, jnp.zeros((D_MODEL, SMALL_COLS - 3 * HEADS), _BF16)], axis=1)
    wsmt = jnp.concatenate(
        [w_beta, w_a, w_a, jnp.zeros((D_MODEL, SMALL_ROWS - 3 * HEADS), _BF16)], axis=1).T
    pad = jnp.zeros((HEADS,), _F32)
    a_log = jnp.concatenate([pad, gdn_a_log[i], gdn_a_log[i]])
    dt_b = jnp.concatenate([pad, gdn_dt_bias[i], gdn_dt_bias[i]])
    smc = jnp.zeros((SUBLANES, SMALL_COLS), _F32)
    smc = smc.at[0, :3 * HEADS].set(a_log).at[1, :3 * HEADS].set(dt_b)
    smct = jnp.zeros((SMALL_ROWS, LANES), _F32)
    smct = smct.at[:3 * HEADS, 0].set(a_log).at[:3 * HEADS, 1].set(dt_b)

    npre = _row(mix_norm_pre[i])
    proj, cv, bg, bgt = _mix_in(
        x2, npre, _column_pieces(wb[:, :o_z]), wb[:, o_a:o_glu], wsm, wsmt, smc, smct,
        _row(cnv_pw1_b[i]), _lane_tiles(cnv_dw_w[i]), batch, seq)
    og = _gdn(proj, bg, bgt, _row(gdn_norm_w[i]), _lane_tiles(gdn_conv_w[i]), batch, seq)
    x2 = _mix_out(x2, og, cv, npre, wb[:, o_glu:], _to_bf16(gdn_w_o[i]), _to_bf16(mix_w_out[i]),
                  _to_bf16(cnv_w_o[i]), _row(cnv_dw_b[i]), _row(cnv_ln_g[i]), _row(cnv_ln_b[i]),
                  _row(cnv_b_o[i]), _row(mix_norm_post[i]))

    x2 = _ffn(x2, _row(ffn2_norm_pre[i]), _row(ffn2_norm_post[i]),
              _to_bf16(ffn2_w_in[i]), _to_bf16(ffn2_w_out[i]))
  return x2.reshape(batch, seq, D_MODEL)
```

```python
import math

import jax
import jax.numpy as jnp
from jax import lax
from jax.experimental import pallas as pl
from jax.experimental.pallas import tpu as pltpu

D_MODEL = 1024
D_FF = 2816
HEADS = 8
HEAD_DIM = 128
GDN_CONV = 4
CNV_CH = 1024
CNV_K = 31
RMS_EPS = 1e-6
LN_EPS = 1e-5

LANES = 128
SUBLANES = 8
VMEM_LIMIT_BYTES = 56 * 1024 * 1024

FFN_ROWS = 512
FF_CHUNK = 2816
IN_ROWS = 512
MIX_ROWS = 256
PROJ_PIECE = 512
CONV_ROWS = 128
OUT_ROWS = 512
CHUNK = 64
QKV_HALO = SUBLANES
GLU_HALO = 4 * SUBLANES
SMALL_COLS = LANES
SMALL_ROWS = 4 * SUBLANES
CNV_TILES = CNV_CH // LANES

_BF16 = jnp.bfloat16
_F32 = jnp.float32


def _dot(a, b):
  return jnp.dot(a, b, preferred_element_type=_F32)


def _dot_nt(a, b):
  return lax.dot_general(a, b, (((1,), (1,)), ((), ())), preferred_element_type=_F32)


def _dot_tn(a, b):
  return lax.dot_general(a, b, (((0,), (0,)), ((), ())), preferred_element_type=_F32)


def _rms(x, w):
  ms = jnp.mean(x * x, axis=-1, keepdims=True)
  return x * lax.rsqrt(ms + RMS_EPS) * w


def _sigmoid(x):
  return 1.0 / (1.0 + jnp.exp(-x))


def _silu(x):
  return x * _sigmoid(x)


def _softplus(x):
  return jnp.maximum(x, 0.0) + jnp.log1p(jnp.exp(-jnp.abs(x)))


def _resident(shape):
  zeros = (0,) * len(shape)
  return pl.BlockSpec(shape, lambda *_: zeros, pipeline_mode=pl.Buffered(1))


def _vregs(x):
  return x.reshape(x.shape[0] // SUBLANES, SUBLANES, LANES)


def _ffn_body(x_ref, npre_ref, npost_ref, win_ref, wout_ref, o_ref):
  x = x_ref[...]
  h = _rms(x, npre_ref[...]).astype(_BF16)
  acc = jnp.zeros((FFN_ROWS, D_MODEL), _F32)
  for c in range(D_FF // FF_CHUNK):
    lo = c * FF_CHUNK
    gate = _dot(h, win_ref[:, lo:lo + FF_CHUNK])
    up = _dot(h, win_ref[:, D_FF + lo:D_FF + lo + FF_CHUNK])
    act = (_silu(gate) * up).astype(_BF16)
    acc = acc + _dot(act, wout_ref[lo:lo + FF_CHUNK, :])
  o_ref[...] = x + 0.5 * _rms(acc, npost_ref[...])


def _ffn(x2, npre, npost, w_in, w_out):
  t = x2.shape[0]
  row = pl.BlockSpec((FFN_ROWS, D_MODEL), lambda i: (i, 0))
  return pl.pallas_call(
      _ffn_body,
      grid=(t // FFN_ROWS,),
      in_specs=[row, _resident((1, D_MODEL)), _resident((1, D_MODEL)),
                _resident((D_MODEL, 2 * D_FF)), _resident((D_FF, D_MODEL))],
      out_specs=row,
      out_shape=jax.ShapeDtypeStruct((t, D_MODEL), _F32),
      compiler_params=pltpu.CompilerParams(
          dimension_semantics=("arbitrary",), vmem_limit_bytes=VMEM_LIMIT_BYTES),
      name="ffn",
  )(x2, npre, npost, w_in, w_out)


def _chunk_masks(n):
  shift = int(math.log2(CHUNK))
  row = lax.broadcasted_iota(jnp.int32, (n, n), 0)
  col = lax.broadcasted_iota(jnp.int32, (n, n), 1)
  same = lax.shift_right_logical(row, shift) == lax.shift_right_logical(col, shift)
  return same & (col <= row), same


def _mix_in_body(x_ref, wcat_ref, wglu_ref, wsm_ref, wsmt_ref, smct_ref, dww_ref,
                 npre_ref, smc_ref, pw1b_ref,
                 proj_ref, cv_ref, bg_ref, bgt_ref,
                 glu_ext, h_scr):
  n = IN_ROWS
  t = pl.program_id(1)

  @pl.when(t == 0)
  def _():
    glu_ext[:, 0:GLU_HALO, :] = jnp.zeros((CNV_TILES, GLU_HALO, LANES), _F32)

  h_scr[...] = _rms(x_ref[...], npre_ref[...]).astype(_BF16)
  h = h_scr[...]

  glu = _dot(h, wglu_ref[...]) + pw1b_ref[...]
  hh = glu[:, :CNV_CH] * _sigmoid(glu[:, CNV_CH:])
  for c in range(CNV_TILES):
    glu_ext[c, GLU_HALO:GLU_HALO + n, :] = hh[:, c * LANES:(c + 1) * LANES]

  tiles_per_piece = PROJ_PIECE // LANES
  reach = SUBLANES * ((CNV_K - 1) // SUBLANES)

  def conv_and_project(c, carry):
    for r in range(0, n, CONV_ROWS):
      acc = jnp.zeros((CONV_ROWS // SUBLANES, SUBLANES, LANES), _F32)
      for s in range(SUBLANES):
        start = GLU_HALO - (CNV_K - 1) + s + r
        window = glu_ext[c, start:start + CONV_ROWS + reach, :]
        for j in range(s, CNV_K, SUBLANES):
          part = _vregs(window[j - s:j - s + CONV_ROWS, :])
          acc = acc + dww_ref[c, j][None] * part
      cv_ref[c, r:r + CONV_ROWS, :] = acc.reshape(CONV_ROWS, LANES)
    glu_ext[c, 0:GLU_HALO, :] = glu_ext[c, n:n + GLU_HALO, :]
    p = _dot(h_scr[...], wcat_ref[c])
    for i in range(tiles_per_piece):
      proj_ref[c * tiles_per_piece + i] = p[:, i * LANES:(i + 1) * LANES]
    return carry

  lax.fori_loop(0, CNV_TILES, conv_and_project, 0)

  m = MIX_ROWS
  causal, same = _chunk_masks(m)
  upper = lax.broadcasted_iota(jnp.int32, (m, m), 0) <= lax.broadcasted_iota(jnp.int32, (m, m), 1)
  upper = (upper & same).astype(_BF16)
  lower_block = jnp.concatenate([causal.astype(_BF16), same.astype(_BF16)], axis=0)
  lane = lax.broadcasted_iota(jnp.int32, (m, SMALL_COLS), 1)
  sub = lax.broadcasted_iota(jnp.int32, (SMALL_ROWS, m), 0)

  def split3(v, axis):
    hi = v.astype(_BF16)
    rest = v - hi.astype(_F32)
    mid = rest.astype(_BF16)
    lo = (rest - mid.astype(_F32)).astype(_BF16)
    return jnp.concatenate([hi, mid, lo], axis=axis)

  for r in range(0, n, m):
    hr = h_scr[r:r + m, :]
    sm = _dot(hr, wsm_ref[...])
    g = -jnp.exp(smc_ref[0:1, :]) * _softplus(sm + smc_ref[1:2, :])
    sums = _dot(lower_block, split3(g, 1))
    sums = sums[:, :SMALL_COLS] + sums[:, SMALL_COLS:2 * SMALL_COLS] + sums[:, 2 * SMALL_COLS:]
    bg_ref[r:r + m, :] = jnp.where(lane < HEADS, _sigmoid(sm),
                                   jnp.where(lane < 2 * HEADS, sums[:m], sums[m:]))
    smt = _dot_nt(wsmt_ref[...], hr)
    gt = -jnp.exp(smct_ref[:, 0:1]) * _softplus(smt + smct_ref[:, 1:2])
    sums_t = _dot(split3(gt, 0), upper)
    sums_t = sums_t[:SMALL_ROWS] + sums_t[SMALL_ROWS:2 * SMALL_ROWS] + sums_t[2 * SMALL_ROWS:]
    bgt_ref[:, r:r + m] = jnp.where(sub < HEADS, _sigmoid(smt), sums_t)


def _mix_in(x2, npre, wcat, wglu, wsm, wsmt, smc, smct, pw1b, dww, batch, seq):
  t = x2.shape[0]
  nt = seq // IN_ROWS
  row = lambda width: pl.BlockSpec((IN_ROWS, width), lambda b, i: (b * nt + i, 0))
  tiles = lambda count: pl.BlockSpec((count, IN_ROWS, LANES), lambda b, i: (0, b * nt + i, 0))
  weights = [wcat, wglu, wsm, wsmt, smct, dww, npre, smc, pw1b]
  out_shape = (
      jax.ShapeDtypeStruct((4 * HEADS, t, LANES), _F32),
      jax.ShapeDtypeStruct((CNV_TILES, t, LANES), _F32),
      jax.ShapeDtypeStruct((t, SMALL_COLS), _F32),
      jax.ShapeDtypeStruct((SMALL_ROWS, t), _F32),
  )
  out_specs = (
      tiles(4 * HEADS), tiles(CNV_TILES), row(SMALL_COLS),
      pl.BlockSpec((SMALL_ROWS, IN_ROWS), lambda b, i: (0, b * nt + i)),
  )
  return pl.pallas_call(
      _mix_in_body,
      grid=(batch, nt),
      in_specs=[row(D_MODEL)] + [_resident(w.shape) for w in weights],
      out_specs=out_specs,
      out_shape=out_shape,
      scratch_shapes=[
          pltpu.VMEM((CNV_TILES, GLU_HALO + IN_ROWS, LANES), _F32),
          pltpu.VMEM((IN_ROWS, D_MODEL), _BF16),
      ],
      compiler_params=pltpu.CompilerParams(
          dimension_semantics=("arbitrary", "arbitrary"), vmem_limit_bytes=VMEM_LIMIT_BYTES),
      name="mix_in",
  )(x2, *weights)


def _gdn_body(q_ref, k_ref, v_ref, z_ref, bg_ref, bgt_ref, nw_ref, convw_ref, o_ref, state, ext):
  n = MIX_ROWS
  t = pl.program_id(1)
  heads = range(HEADS)

  @pl.when(t == 0)
  def _():
    state[...] = jnp.zeros((HEADS, HEAD_DIM, HEAD_DIM), _F32)
    ext[:, 0:QKV_HALO, :] = jnp.zeros((3 * HEADS, QKV_HALO, LANES), _F32)

  def conv(part, ref, j):
    c = part * HEADS + j
    ext[c, QKV_HALO:QKV_HALO + n, :] = ref[j]
    acc = jnp.zeros((n // SUBLANES, SUBLANES, LANES), _F32)
    for tap in range(GDN_CONV):
      start = QKV_HALO - (GDN_CONV - 1) + tap
      acc = acc + convw_ref[c, tap][None] * _vregs(ext[c, start:start + n, :])
    ext[c, 0:QKV_HALO, :] = ext[c, n:n + QKV_HALO, :]
    y = _silu(acc.reshape(n, LANES))
    if part < 2:
      y = y * lax.rsqrt(jnp.sum(y * y, axis=-1, keepdims=True) + 1e-6)
    if part == 0:
      y = y * (HEAD_DIM ** -0.5)
    return y

  bg = bg_ref[...]
  causal, _ = _chunk_masks(n)
  row = lax.broadcasted_iota(jnp.int32, (n, n), 0)
  col = lax.broadcasted_iota(jnp.int32, (n, n), 1)
  strict = causal & (col < row)
  eye = (row == col).astype(_F32)

  beta = [bg[:, j:j + 1] for j in heads]
  g_col = [bg[:, HEADS + j:HEADS + j + 1] for j in heads]
  g_last = [bg[:, 2 * HEADS + j:2 * HEADS + j + 1] for j in heads]
  qs, ks, vs, qkb, xs, ps = [], [], [], [], [], []
  for j in heads:
    q = conv(0, q_ref, j)
    k = conv(1, k_ref, j)
    v = conv(2, v_ref, j)
    g_row = bgt_ref[HEADS + j:HEADS + j + 1, :]
    d = jnp.where(causal, jnp.exp(jnp.where(causal, g_col[j] - g_row, 0.0)), 0.0)
    kb = k.astype(_BF16)
    qk = (_dot_nt(q.astype(_BF16), kb) * d).astype(_BF16)
    qkb.append([qk[c * CHUNK:(c + 1) * CHUNK, c * CHUNK:(c + 1) * CHUNK]
                for c in range(n // CHUNK)])
    a_mat = jnp.where(strict, _dot_nt(kb, kb) * d * beta[j], 0.0)
    xs.append(eye - a_mat)
    ps.append(a_mat)
    qs.append(q)
    ks.append(k)
    vs.append(v)

  for _ in range(int(math.log2(CHUNK)) - 1):
    for j in heads:
      pb = ps[j].astype(_BF16)
      ps[j] = _dot(pb, pb)
      xs[j] = xs[j] + _dot(xs[j].astype(_BF16), ps[j].astype(_BF16))

  us, ws, q_dec, k_dec = [], [], [], []
  for j in heads:
    e_g = jnp.exp(g_col[j])
    rhs = jnp.concatenate([vs[j] * beta[j], ks[j] * (beta[j] * e_g)], axis=-1).astype(_BF16)
    uw = _dot(xs[j].astype(_BF16), rhs)
    us.append(uw[:, :HEAD_DIM])
    ws.append(uw[:, HEAD_DIM:])
    q_dec.append(qs[j] * e_g)
    k_dec.append((ks[j] * jnp.exp(g_last[j] - g_col[j])).astype(_BF16))

  s = [state[j] for j in heads]
  for c in range(n // CHUNK):
    r = slice(c * CHUNK, (c + 1) * CHUNK)
    ws_qs = []
    for j in heads:
      wq = jnp.concatenate([ws[j][r], q_dec[j][r]], axis=0).astype(_BF16)
      ws_qs.append(_dot(wq, s[j].astype(_BF16)))
    for j in heads:
      v_new = (us[j][r] - ws_qs[j][:CHUNK]).astype(_BF16)
      o_c = ws_qs[j][CHUNK:] + _dot(qkb[j][c], v_new)
      d_c = jnp.exp(g_last[j][c * CHUNK:c * CHUNK + 1, :])
      s[j] = s[j] * d_c + _dot_tn(k_dec[j][r], v_new)
      o_n = _rms(o_c, nw_ref[...]) * _silu(z_ref[j, r, :])
      o_ref[r, j * HEAD_DIM:(j + 1) * HEAD_DIM] = o_n.astype(o_ref.dtype)
  for j in heads:
    state[j] = s[j]


def _gdn(proj, bg, bgt, norm_w, convw, batch, seq):
  t = bg.shape[0]
  nt = seq // MIX_ROWS

  def head_major(part):
    return pl.BlockSpec((HEADS, MIX_ROWS, LANES), lambda b, i: (part, b * nt + i, 0))

  return pl.pallas_call(
      _gdn_body,
      grid=(batch, nt),
      in_specs=[head_major(0), head_major(1), head_major(2), head_major(3),
                pl.BlockSpec((MIX_ROWS, SMALL_COLS), lambda b, i: (b * nt + i, 0)),
                pl.BlockSpec((SMALL_ROWS, MIX_ROWS), lambda b, i: (0, b * nt + i)),
                _resident((1, HEAD_DIM)), _resident(convw.shape)],
      out_specs=pl.BlockSpec((MIX_ROWS, D_MODEL), lambda b, i: (b * nt + i, 0)),
      out_shape=jax.ShapeDtypeStruct((t, D_MODEL), _BF16),
      scratch_shapes=[pltpu.VMEM((HEADS, HEAD_DIM, HEAD_DIM), _F32),
                      pltpu.VMEM((3 * HEADS, QKV_HALO + MIX_ROWS, LANES), _F32)],
      compiler_params=pltpu.CompilerParams(
          dimension_semantics=("arbitrary", "arbitrary"), vmem_limit_bytes=VMEM_LIMIT_BYTES),
      name="gdn",
  )(proj, proj, proj, proj, bg, bgt, norm_w, convw)


def _mix_out_body(x_ref, og_ref, cv_ref, npre_ref, wgat_ref, wgo_ref, wmo_ref, cwo_ref, dwb_ref,
                  lng_ref, lnb_ref, cbo_ref, npost_ref, o_ref):
  x = x_ref[...]
  gates = _sigmoid(_dot(_rms(x, npre_ref[...]).astype(_BF16), wgat_ref[...]))
  cv = jnp.concatenate([cv_ref[c] for c in range(CNV_TILES)], axis=-1) + dwb_ref[...]
  mu = jnp.mean(cv, axis=-1, keepdims=True)
  cen = cv - mu
  var = jnp.mean(cen * cen, axis=-1, keepdims=True)
  ln = cen * lax.rsqrt(var + LN_EPS) * lng_ref[...] + lnb_ref[...]
  y_b = _dot(_silu(ln).astype(_BF16), cwo_ref[...]) + cbo_ref[...]
  y_a = _dot(og_ref[...], wgo_ref[...])
  y = gates[:, :D_MODEL] * y_a + gates[:, D_MODEL:] * y_b
  y2 = _dot(y.astype(_BF16), wmo_ref[...])
  o_ref[...] = x + _rms(y2, npost_ref[...])


def _mix_out(x2, og, cv, npre, wgat, wgo, wmo, cwo, dwb, lng, lnb, cbo, npost):
  t = x2.shape[0]
  row = pl.BlockSpec((OUT_ROWS, D_MODEL), lambda i: (i, 0))
  square = _resident((D_MODEL, D_MODEL))
  vec = _resident((1, D_MODEL))
  return pl.pallas_call(
      _mix_out_body,
      grid=(t // OUT_ROWS,),
      in_specs=[row, row, pl.BlockSpec((CNV_TILES, OUT_ROWS, LANES), lambda i: (0, i, 0)),
                vec, _resident((D_MODEL, 2 * D_MODEL)), square, square, square,
                vec, vec, vec, vec, vec],
      out_specs=row,
      out_shape=jax.ShapeDtypeStruct((t, D_MODEL), _F32),
      compiler_params=pltpu.CompilerParams(
          dimension_semantics=("arbitrary",), vmem_limit_bytes=VMEM_LIMIT_BYTES),
      name="mix_out",
  )(x2, og, cv, npre, wgat, wgo, wmo, cwo, dwb, lng, lnb, cbo, npost)


def _row(v):
  return v.reshape(1, -1).astype(_F32)


def _column_pieces(w):
  k, c = w.shape
  return w.reshape(k, c // PROJ_PIECE, PROJ_PIECE).transpose(1, 0, 2)


def _lane_tiles(w):
  k, c = w.shape
  tiles = w.reshape(k, c // LANES, LANES).transpose(1, 0, 2).astype(_F32)
  return jnp.broadcast_to(tiles[:, :, None, :], (c // LANES, k, SUBLANES, LANES))


def kernel(x, ffn1_norm_pre, ffn1_norm_post, ffn1_w_in, ffn1_w_out, mix_norm_pre, mix_norm_post,
           mix_w_in, gdn_conv_w, gdn_a_log, gdn_dt_bias, gdn_norm_w, gdn_w_o, cnv_pw1_b, cnv_dw_w,
           cnv_dw_b, cnv_ln_g, cnv_ln_b, cnv_w_o, cnv_b_o, mix_w_out, ffn2_norm_pre,
           ffn2_norm_post, ffn2_w_in, ffn2_w_out):
  batch, seq, _ = x.shape
  depth = ffn1_w_in.shape[0]
  x2 = x.reshape(batch * seq, D_MODEL)
  o_qkv = 3 * HEADS * HEAD_DIM
  o_z = o_qkv + HEADS * HEAD_DIM
  o_b = o_z + HEADS
  o_a = o_b + HEADS
  o_glu = o_a + 2 * CNV_CH
  for i in range(depth):
    x2 = _ffn(x2, _row(ffn1_norm_pre[i]), _row(ffn1_norm_post[i]),
              ffn1_w_in[i].astype(_BF16), ffn1_w_out[i].astype(_BF16))

    w = mix_w_in[i]
    w_beta = w[:, o_z:o_b]
    w_a = w[:, o_b:o_a]
    wsm = jnp.concatenate(
        [w_beta, w_a, w_a, jnp.zeros((D_MODEL, SMALL_COLS - 3 * HEADS), _F32)], axis=1)
    wsmt = jnp.concatenate(
        [w_beta, w_a, w_a, jnp.zeros((D_MODEL, SMALL_ROWS - 3 * HEADS), _F32)], axis=1).T
    pad = jnp.zeros((HEADS,), _F32)
    a_log = jnp.concatenate([pad, gdn_a_log[i], gdn_a_log[i]])
    dt_b = jnp.concatenate([pad, gdn_dt_bias[i], gdn_dt_bias[i]])
    smc = jnp.zeros((SUBLANES, SMALL_COLS), _F32)
    smc = smc.at[0, :3 * HEADS].set(a_log).at[1, :3 * HEADS].set(dt_b)
    smct = jnp.zeros((SMALL_ROWS, LANES), _F32)
    smct = smct.at[:3 * HEADS, 0].set(a_log).at[:3 * HEADS, 1].set(dt_b)

    npre = _row(mix_norm_pre[i])
    proj, cv, bg, bgt = _mix_in(
        x2, npre, _column_pieces(w[:, :o_z].astype(_BF16)), w[:, o_a:o_glu].astype(_BF16),
        wsm.astype(_BF16), wsmt.astype(_BF16), smc, smct, _row(cnv_pw1_b[i]),
        _lane_tiles(cnv_dw_w[i]), batch, seq)
    og = _gdn(proj, bg, bgt, _row(gdn_norm_w[i]), _lane_tiles(gdn_conv_w[i]), batch, seq)
    x2 = _mix_out(x2, og, cv, npre, w[:, o_glu:].astype(_BF16), gdn_w_o[i].astype(_BF16),
                  mix_w_out[i].astype(_BF16), cnv_w_o[i].astype(_BF16), _row(cnv_dw_b[i]),
                  _row(cnv_ln_g[i]), _row(cnv_ln_b[i]), _row(cnv_b_o[i]), _row(mix_norm_post[i]))

    x2 = _ffn(x2, _row(ffn2_norm_pre[i]), _row(ffn2_norm_post[i]),
              ffn2_w_in[i].astype(_BF16), ffn2_w_out[i].astype(_BF16))
  return x2.reshape(batch, seq, D_MODEL)
```

```python
import math

import jax
import jax.numpy as jnp
from jax import lax
from jax.experimental import pallas as pl
from jax.experimental.pallas import tpu as pltpu

D_MODEL = 1024
D_FF = 2816
HEADS = 8
HEAD_DIM = 128
GDN_CONV = 4
CNV_CH = 1024
CNV_K = 31
RMS_EPS = 1e-6
LN_EPS = 1e-5

LANES = 128
SUBLANES = 8
VMEM_LIMIT_BYTES = 56 * 1024 * 1024

FFN_ROWS = 512
FF_CHUNK = 2816
IN_ROWS = 512
MIX_ROWS = 256
PROJ_PIECE = 512
CONV_ROWS = 128
OUT_ROWS = 512
CHUNK = 64
QKV_HALO = SUBLANES
GLU_HALO = 4 * SUBLANES
SMALL_COLS = LANES
SMALL_ROWS = 4 * SUBLANES
CNV_TILES = CNV_CH // LANES

_BF16 = jnp.bfloat16
_F32 = jnp.float32


def _dot(a, b):
  return jnp.dot(a, b, preferred_element_type=_F32)


def _dot_nt(a, b):
  return lax.dot_general(a, b, (((1,), (1,)), ((), ())), preferred_element_type=_F32)


def _dot_tn(a, b):
  return lax.dot_general(a, b, (((0,), (0,)), ((), ())), preferred_element_type=_F32)


def _rms(x, w):
  ms = jnp.mean(x * x, axis=-1, keepdims=True)
  return x * lax.rsqrt(ms + RMS_EPS) * w


def _sigmoid(x):
  return 1.0 / (1.0 + jnp.exp(-x))


def _silu(x):
  return x * _sigmoid(x)


def _softplus(x):
  return jnp.maximum(x, 0.0) + jnp.log1p(jnp.exp(-jnp.abs(x)))


def _resident(shape):
  zeros = (0,) * len(shape)
  return pl.BlockSpec(shape, lambda *_: zeros, pipeline_mode=pl.Buffered(1))


def _vregs(x):
  return x.reshape(x.shape[0] // SUBLANES, SUBLANES, LANES)


def _ffn_body(x_ref, npre_ref, npost_ref, win_ref, wout_ref, o_ref):
  x = x_ref[...]
  h = _rms(x, npre_ref[...]).astype(_BF16)
  acc = jnp.zeros((FFN_ROWS, D_MODEL), _F32)
  for c in range(D_FF // FF_CHUNK):
    lo = c * FF_CHUNK
    gate = _dot(h, win_ref[:, lo:lo + FF_CHUNK])
    up = _dot(h, win_ref[:, D_FF + lo:D_FF + lo + FF_CHUNK])
    act = (_silu(gate) * up).astype(_BF16)
    acc = acc + _dot(act, wout_ref[lo:lo + FF_CHUNK, :])
  o_ref[...] = x + 0.5 * _rms(acc, npost_ref[...])


def _ffn(x2, npre, npost, w_in, w_out):
  t = x2.shape[0]
  row = pl.BlockSpec((FFN_ROWS, D_MODEL), lambda i: (i, 0))
  return pl.pallas_call(
      _ffn_body,
      grid=(t // FFN_ROWS,),
      in_specs=[row, _resident((1, D_MODEL)), _resident((1, D_MODEL)),
                _resident((D_MODEL, 2 * D_FF)), _resident((D_FF, D_MODEL))],
      out_specs=row,
      out_shape=jax.ShapeDtypeStruct((t, D_MODEL), _F32),
      compiler_params=pltpu.CompilerParams(
          dimension_semantics=("arbitrary",), vmem_limit_bytes=VMEM_LIMIT_BYTES),
      name="ffn",
  )(x2, npre, npost, w_in, w_out)


def _chunk_masks(n):
  shift = int(math.log2(CHUNK))
  row = lax.broadcasted_iota(jnp.int32, (n, n), 0)
  col = lax.broadcasted_iota(jnp.int32, (n, n), 1)
  same = lax.shift_right_logical(row, shift) == lax.shift_right_logical(col, shift)
  return same & (col <= row), same


def _mix_in_body(x_ref, wcat_ref, wglu_ref, wsm_ref, wsmt_ref, smct_ref, dww_ref,
                 npre_ref, smc_ref, pw1b_ref,
                 proj_ref, cv_ref, bg_ref, bgt_ref,
                 glu_ext, h_scr):
  n = IN_ROWS
  t = pl.program_id(1)

  @pl.when(t == 0)
  def _():
    glu_ext[:, 0:GLU_HALO, :] = jnp.zeros((CNV_TILES, GLU_HALO, LANES), _F32)

  h_scr[...] = _rms(x_ref[...], npre_ref[...]).astype(_BF16)
  h = h_scr[...]

  glu = _dot(h, wglu_ref[...]) + pw1b_ref[...]
  hh = glu[:, :CNV_CH] * _sigmoid(glu[:, CNV_CH:])
  for c in range(CNV_TILES):
    glu_ext[c, GLU_HALO:GLU_HALO + n, :] = hh[:, c * LANES:(c + 1) * LANES]

  tiles_per_piece = PROJ_PIECE // LANES
  reach = SUBLANES * ((CNV_K - 1) // SUBLANES)

  def conv_tile(c, carry):
    for r in range(0, n, CONV_ROWS):
      acc = jnp.zeros((CONV_ROWS // SUBLANES, SUBLANES, LANES), _F32)
      for s in range(SUBLANES):
        start = GLU_HALO - (CNV_K - 1) + s + r
        window = glu_ext[c, start:start + CONV_ROWS + reach, :]
        for j in range(s, CNV_K, SUBLANES):
          part = _vregs(window[j - s:j - s + CONV_ROWS, :])
          acc = acc + dww_ref[c, j][None] * part
      cv_ref[c, r:r + CONV_ROWS, :] = acc.reshape(CONV_ROWS, LANES)
    glu_ext[c, 0:GLU_HALO, :] = glu_ext[c, n:n + GLU_HALO, :]
    return carry

  for c in range(wcat_ref.shape[0]):
    p = _dot(h, wcat_ref[c])
    for i in range(tiles_per_piece):
      proj_ref[c * tiles_per_piece + i] = p[:, i * LANES:(i + 1) * LANES]

  lax.fori_loop(0, CNV_TILES, conv_tile, 0)

  m = MIX_ROWS
  causal, same = _chunk_masks(m)
  upper = lax.broadcasted_iota(jnp.int32, (m, m), 0) <= lax.broadcasted_iota(jnp.int32, (m, m), 1)
  upper = (upper & same).astype(_BF16)
  lower_block = jnp.concatenate([causal.astype(_BF16), same.astype(_BF16)], axis=0)
  lane = lax.broadcasted_iota(jnp.int32, (m, SMALL_COLS), 1)
  sub = lax.broadcasted_iota(jnp.int32, (SMALL_ROWS, m), 0)

  def split3(v, axis):
    hi = v.astype(_BF16)
    rest = v - hi.astype(_F32)
    mid = rest.astype(_BF16)
    lo = (rest - mid.astype(_F32)).astype(_BF16)
    return jnp.concatenate([hi, mid, lo], axis=axis)

  for r in range(0, n, m):
    hr = h_scr[r:r + m, :]
    sm = _dot(hr, wsm_ref[...])
    g = -jnp.exp(smc_ref[0:1, :]) * _softplus(sm + smc_ref[1:2, :])
    sums = _dot(lower_block, split3(g, 1))
    sums = sums[:, :SMALL_COLS] + sums[:, SMALL_COLS:2 * SMALL_COLS] + sums[:, 2 * SMALL_COLS:]
    bg_ref[r:r + m, :] = jnp.where(lane < HEADS, _sigmoid(sm),
                                   jnp.where(lane < 2 * HEADS, sums[:m], sums[m:]))
    smt = _dot_nt(wsmt_ref[...], hr)
    gt = -jnp.exp(smct_ref[:, 0:1]) * _softplus(smt + smct_ref[:, 1:2])
    sums_t = _dot(split3(gt, 0), upper)
    sums_t = sums_t[:SMALL_ROWS] + sums_t[SMALL_ROWS:2 * SMALL_ROWS] + sums_t[2 * SMALL_ROWS:]
    bgt_ref[:, r:r + m] = jnp.where(sub < HEADS, _sigmoid(smt), sums_t)


def _mix_in(x2, npre, wcat, wglu, wsm, wsmt, smc, smct, pw1b, dww, batch, seq):
  t = x2.shape[0]
  nt = seq // IN_ROWS
  row = lambda width: pl.BlockSpec((IN_ROWS, width), lambda b, i: (b * nt + i, 0))
  tiles = lambda count: pl.BlockSpec((count, IN_ROWS, LANES), lambda b, i: (0, b * nt + i, 0))
  weights = [wcat, wglu, wsm, wsmt, smct, dww, npre, smc, pw1b]
  out_shape = (
      jax.ShapeDtypeStruct((4 * HEADS, t, LANES), _F32),
      jax.ShapeDtypeStruct((CNV_TILES, t, LANES), _F32),
      jax.ShapeDtypeStruct((t, SMALL_COLS), _F32),
      jax.ShapeDtypeStruct((SMALL_ROWS, t), _F32),
  )
  out_specs = (
      tiles(4 * HEADS), tiles(CNV_TILES), row(SMALL_COLS),
      pl.BlockSpec((SMALL_ROWS, IN_ROWS), lambda b, i: (0, b * nt + i)),
  )
  return pl.pallas_call(
      _mix_in_body,
      grid=(batch, nt),
      in_specs=[row(D_MODEL)] + [_resident(w.shape) for w in weights],
      out_specs=out_specs,
      out_shape=out_shape,
      scratch_shapes=[
          pltpu.VMEM((CNV_TILES, GLU_HALO + IN_ROWS, LANES), _F32),
          pltpu.VMEM((IN_ROWS, D_MODEL), _BF16),
      ],
      compiler_params=pltpu.CompilerParams(
          dimension_semantics=("arbitrary", "arbitrary"), vmem_limit_bytes=VMEM_LIMIT_BYTES),
      name="mix_in",
  )(x2, *weights)


def _gdn_body(q_ref, k_ref, v_ref, z_ref, bg_ref, bgt_ref, nw_ref, convw_ref, o_ref, state, ext):
  n = MIX_ROWS
  t = pl.program_id(1)
  heads = range(HEADS)

  @pl.when(t == 0)
  def _():
    state[...] = jnp.zeros((HEADS, HEAD_DIM, HEAD_DIM), _F32)
    ext[:, 0:QKV_HALO, :] = jnp.zeros((3 * HEADS, QKV_HALO, LANES), _F32)

  def conv(part, ref, j):
    c = part * HEADS + j
    ext[c, QKV_HALO:QKV_HALO + n, :] = ref[j]
    acc = jnp.zeros((n // SUBLANES, SUBLANES, LANES), _F32)
    for tap in range(GDN_CONV):
      start = QKV_HALO - (GDN_CONV - 1) + tap
      acc = acc + convw_ref[c, tap][None] * _vregs(ext[c, start:start + n, :])
    ext[c, 0:QKV_HALO, :] = ext[c, n:n + QKV_HALO, :]
    y = _silu(acc.reshape(n, LANES))
    if part < 2:
      y = y * lax.rsqrt(jnp.sum(y * y, axis=-1, keepdims=True) + 1e-6)
    if part == 0:
      y = y * (HEAD_DIM ** -0.5)
    return y

  bg = bg_ref[...]
  causal, _ = _chunk_masks(n)
  row = lax.broadcasted_iota(jnp.int32, (n, n), 0)
  col = lax.broadcasted_iota(jnp.int32, (n, n), 1)
  strict = causal & (col < row)
  eye = (row == col).astype(_F32)

  beta = [bg[:, j:j + 1] for j in heads]
  g_col = [bg[:, HEADS + j:HEADS + j + 1] for j in heads]
  g_last = [bg[:, 2 * HEADS + j:2 * HEADS + j + 1] for j in heads]
  qs, ks, vs, qkb, xs, ps = [], [], [], [], [], []
  for j in heads:
    q = conv(0, q_ref, j)
    k = conv(1, k_ref, j)
    v = conv(2, v_ref, j)
    g_row = bgt_ref[HEADS + j:HEADS + j + 1, :]
    d = jnp.where(causal, jnp.exp(jnp.where(causal, g_col[j] - g_row, 0.0)), 0.0)
    kb = k.astype(_BF16)
    qk = (_dot_nt(q.astype(_BF16), kb) * d).astype(_BF16)
    qkb.append([qk[c * CHUNK:(c + 1) * CHUNK, c * CHUNK:(c + 1) * CHUNK]
                for c in range(n // CHUNK)])
    a_mat = jnp.where(strict, _dot_nt(kb, kb) * d * beta[j], 0.0)
    xs.append(eye - a_mat)
    ps.append(a_mat)
    qs.append(q)
    ks.append(k)
    vs.append(v)

  for _ in range(int(math.log2(CHUNK)) - 1):
    for j in heads:
      pb = ps[j].astype(_BF16)
      ps[j] = _dot(pb, pb)
      xs[j] = xs[j] + _dot(xs[j].astype(_BF16), ps[j].astype(_BF16))

  us, ws, q_dec, k_dec = [], [], [], []
  for j in heads:
    e_g = jnp.exp(g_col[j])
    rhs = jnp.concatenate([vs[j] * beta[j], ks[j] * (beta[j] * e_g)], axis=-1).astype(_BF16)
    uw = _dot(xs[j].astype(_BF16), rhs)
    us.append(uw[:, :HEAD_DIM])
    ws.append(uw[:, HEAD_DIM:])
    q_dec.append(qs[j] * e_g)
    k_dec.append((ks[j] * jnp.exp(g_last[j] - g_col[j])).astype(_BF16))

  s = [state[j] for j in heads]
  for c in range(n // CHUNK):
    r = slice(c * CHUNK, (c + 1) * CHUNK)
    ws_qs = []
    for j in heads:
      wq = jnp.concatenate([ws[j][r], q_dec[j][r]], axis=0).astype(_BF16)
      ws_qs.append(_dot(wq, s[j].astype(_BF16)))
    for j in heads:
      v_new = (us[j][r] - ws_qs[j][:CHUNK]).astype(_BF16)
      o_c = ws_qs[j][CHUNK:] + _dot(qkb[j][c], v_new)
      d_c = jnp.exp(g_last[j][c * CHUNK:c * CHUNK + 1, :])
      s[j] = s[j] * d_c + _dot_tn(k_dec[j][r], v_new)
      o_n = _rms(o_c, nw_ref[...]) * _silu(z_ref[j, r, :])
      o_ref[r, j * HEAD_DIM:(j + 1) * HEAD_DIM] = o_n.astype(o_ref.dtype)
  for j in heads:
    state[j] = s[j]


def _gdn(proj, bg, bgt, norm_w, convw, batch, seq):
  t = bg.shape[0]
  nt = seq // MIX_ROWS

  def head_major(part):
    return pl.BlockSpec((HEADS, MIX_ROWS, LANES), lambda b, i: (part, b * nt + i, 0))

  return pl.pallas_call(
      _gdn_body,
      grid=(batch, nt),
      in_specs=[head_major(0), head_major(1), head_major(2), head_major(3),
                pl.BlockSpec((MIX_ROWS, SMALL_COLS), lambda b, i: (b * nt + i, 0)),
                pl.BlockSpec((SMALL_ROWS, MIX_ROWS), lambda b, i: (0, b * nt + i)),
                _resident((1, HEAD_DIM)), _resident(convw.shape)],
      out_specs=pl.BlockSpec((MIX_ROWS, D_MODEL), lambda b, i: (b * nt + i, 0)),
      out_shape=jax.ShapeDtypeStruct((t, D_MODEL), _BF16),
      scratch_shapes=[pltpu.VMEM((HEADS, HEAD_DIM, HEAD_DIM), _F32),
                      pltpu.VMEM((3 * HEADS, QKV_HALO + MIX_ROWS, LANES), _F32)],
      compiler_params=pltpu.CompilerParams(
          dimension_semantics=("arbitrary", "arbitrary"), vmem_limit_bytes=VMEM_LIMIT_BYTES),
      name="gdn",
  )(proj, proj, proj, proj, bg, bgt, norm_w, convw)


def _mix_out_body(x_ref, og_ref, cv_ref, npre_ref, wgat_ref, wgo_ref, wmo_ref, cwo_ref, dwb_ref,
                  lng_ref, lnb_ref, cbo_ref, npost_ref, o_ref):
  x = x_ref[...]
  gates = _sigmoid(_dot(_rms(x, npre_ref[...]).astype(_BF16), wgat_ref[...]))
  cv = jnp.concatenate([cv_ref[c] for c in range(CNV_TILES)], axis=-1) + dwb_ref[...]
  mu = jnp.mean(cv, axis=-1, keepdims=True)
  cen = cv - mu
  var = jnp.mean(cen * cen, axis=-1, keepdims=True)
  ln = cen * lax.rsqrt(var + LN_EPS) * lng_ref[...] + lnb_ref[...]
  y_b = _dot(_silu(ln).astype(_BF16), cwo_ref[...]) + cbo_ref[...]
  y_a = _dot(og_ref[...], wgo_ref[...])
  y = gates[:, :D_MODEL] * y_a + gates[:, D_MODEL:] * y_b
  y2 = _dot(y.astype(_BF16), wmo_ref[...])
  o_ref[...] = x + _rms(y2, npost_ref[...])


def _mix_out(x2, og, cv, npre, wgat, wgo, wmo, cwo, dwb, lng, lnb, cbo, npost):
  t = x2.shape[0]
  row = pl.BlockSpec((OUT_ROWS, D_MODEL), lambda i: (i, 0))
  square = _resident((D_MODEL, D_MODEL))
  vec = _resident((1, D_MODEL))
  return pl.pallas_call(
      _mix_out_body,
      grid=(t // OUT_ROWS,),
      in_specs=[row, row, pl.BlockSpec((CNV_TILES, OUT_ROWS, LANES), lambda i: (0, i, 0)),
                vec, _resident((D_MODEL, 2 * D_MODEL)), square, square, square,
                vec, vec, vec, vec, vec],
      out_specs=row,
      out_shape=jax.ShapeDtypeStruct((t, D_MODEL), _F32),
      compiler_params=pltpu.CompilerParams(
          dimension_semantics=("arbitrary",), vmem_limit_bytes=VMEM_LIMIT_BYTES),
      name="mix_out",
  )(x2, og, cv, npre, wgat, wgo, wmo, cwo, dwb, lng, lnb, cbo, npost)


def _row(v):
  return v.reshape(1, -1).astype(_F32)


def _column_pieces(w):
  k, c = w.shape
  return w.reshape(k, c // PROJ_PIECE, PROJ_PIECE).transpose(1, 0, 2)


def _lane_tiles(w):
  k, c = w.shape
  tiles = w.reshape(k, c // LANES, LANES).transpose(1, 0, 2).astype(_F32)
  return jnp.broadcast_to(tiles[:, :, None, :], (c // LANES, k, SUBLANES, LANES))


def kernel(x, ffn1_norm_pre, ffn1_norm_post, ffn1_w_in, ffn1_w_out, mix_norm_pre, mix_norm_post,
           mix_w_in, gdn_conv_w, gdn_a_log, gdn_dt_bias, gdn_norm_w, gdn_w_o, cnv_pw1_b, cnv_dw_w,
           cnv_dw_b, cnv_ln_g, cnv_ln_b, cnv_w_o, cnv_b_o, mix_w_out, ffn2_norm_pre,
           ffn2_norm_post, ffn2_w_in, ffn2_w_out):
  batch, seq, _ = x.shape
  depth = ffn1_w_in.shape[0]
  x2 = x.reshape(batch * seq, D_MODEL)
  o_qkv = 3 * HEADS * HEAD_DIM
  o_z = o_qkv + HEADS * HEAD_DIM
  o_b = o_z + HEADS
  o_a = o_b + HEADS
  o_glu = o_a + 2 * CNV_CH
  for i in range(depth):
    x2 = _ffn(x2, _row(ffn1_norm_pre[i]), _row(ffn1_norm_post[i]),
              ffn1_w_in[i].astype(_BF16), ffn1_w_out[i].astype(_BF16))

    w = mix_w_in[i]
    w_beta = w[:, o_z:o_b]
    w_a = w[:, o_b:o_a]
    wsm = jnp.concatenate(
        [w_beta, w_a, w_a, jnp.zeros((D_MODEL, SMALL_COLS - 3 * HEADS), _F32)], axis=1)
    wsmt = jnp.concatenate(
        [w_beta, w_a, w_a, jnp.zeros((D_MODEL, SMALL_ROWS - 3 * HEADS), _F32)], axis=1).T
    pad = jnp.zeros((HEADS,), _F32)
    a_log = jnp.concatenate([pad, gdn_a_log[i], gdn_a_log[i]])
    dt_b = jnp.concatenate([pad, gdn_dt_bias[i], gdn_dt_bias[i]])
    smc = jnp.zeros((SUBLANES, SMALL_COLS), _F32)
    smc = smc.at[0, :3 * HEADS].set(a_log).at[1, :3 * HEADS].set(dt_b)
    smct = jnp.zeros((SMALL_ROWS, LANES), _F32)
    smct = smct.at[:3 * HEADS, 0].set(a_log).at[:3 * HEADS, 1].set(dt_b)

    npre = _row(mix_norm_pre[i])
    proj, cv, bg, bgt = _mix_in(
        x2, npre, _column_pieces(w[:, :o_z].astype(_BF16)), w[:, o_a:o_glu].astype(_BF16),
        wsm.astype(_BF16), wsmt.astype(_BF16), smc, smct, _row(cnv_pw1_b[i]),
        _lane_tiles(cnv_dw_w[i]), batch, seq)
    og = _gdn(proj, bg, bgt, _row(gdn_norm_w[i]), _lane_tiles(gdn_conv_w[i]), batch, seq)
    x2 = _mix_out(x2, og, cv, npre, w[:, o_glu:].astype(_BF16), gdn_w_o[i].astype(_BF16),
                  mix_w_out[i].astype(_BF16), cnv_w_o[i].astype(_BF16), _row(cnv_dw_b[i]),
                  _row(cnv_ln_g[i]), _row(cnv_ln_b[i]), _row(cnv_b_o[i]), _row(mix_norm_post[i]))

    x2 = _ffn(x2, _row(ffn2_norm_pre[i]), _row(ffn2_norm_post[i]),
              ffn2_w_in[i].astype(_BF16), ffn2_w_out[i].astype(_BF16))
  return x2.reshape(batch, seq, D_MODEL)
```

```python
import math

import jax
import jax.numpy as jnp
from jax import lax
from jax.experimental import pallas as pl
from jax.experimental.pallas import tpu as pltpu

D_MODEL = 1024
D_FF = 2816
HEADS = 8
HEAD_DIM = 128
GDN_CONV = 4
CNV_CH = 1024
CNV_K = 31
RMS_EPS = 1e-6
LN_EPS = 1e-5

LANES = 128
SUBLANES = 8
VMEM_LIMIT_BYTES = 56 * 1024 * 1024

FFN_ROWS = 512
FF_CHUNK = 2816
IN_ROWS = 512
MIX_ROWS = 256
PROJ_PIECE = 512
CONV_ROWS = 128
OUT_ROWS = 512
CHUNK = 64
GDN_BATCH = 4
QKV_HALO = SUBLANES
GLU_HALO = 4 * SUBLANES
SMALL_COLS = LANES
SMALL_ROWS = 4 * SUBLANES
CNV_TILES = CNV_CH // LANES

_BF16 = jnp.bfloat16
_F32 = jnp.float32


def _dot(a, b):
  return jnp.dot(a, b, preferred_element_type=_F32)


def _dot_nt(a, b):
  return lax.dot_general(a, b, (((1,), (1,)), ((), ())), preferred_element_type=_F32)


def _dot_tn(a, b):
  return lax.dot_general(a, b, (((0,), (0,)), ((), ())), preferred_element_type=_F32)


def _rms(x, w):
  ms = jnp.mean(x * x, axis=-1, keepdims=True)
  return x * lax.rsqrt(ms + RMS_EPS) * w


def _sigmoid(x):
  return 1.0 / (1.0 + jnp.exp(-x))


def _silu(x):
  return x * _sigmoid(x)


def _softplus(x):
  return jnp.maximum(x, 0.0) + jnp.log1p(jnp.exp(-jnp.abs(x)))


def _resident(shape):
  zeros = (0,) * len(shape)
  return pl.BlockSpec(shape, lambda *_: zeros, pipeline_mode=pl.Buffered(1))


def _vregs(x):
  return x.reshape(x.shape[0] // SUBLANES, SUBLANES, LANES)


def _ffn_body(x_ref, npre_ref, npost_ref, win_ref, wout_ref, o_ref):
  x = x_ref[...]
  h = _rms(x, npre_ref[...]).astype(_BF16)
  acc = jnp.zeros((FFN_ROWS, D_MODEL), _F32)
  for c in range(D_FF // FF_CHUNK):
    lo = c * FF_CHUNK
    gate = _dot(h, win_ref[:, lo:lo + FF_CHUNK])
    up = _dot(h, win_ref[:, D_FF + lo:D_FF + lo + FF_CHUNK])
    act = (_silu(gate) * up).astype(_BF16)
    acc = acc + _dot(act, wout_ref[lo:lo + FF_CHUNK, :])
  o_ref[...] = x + 0.5 * _rms(acc, npost_ref[...])


def _ffn(x2, npre, npost, w_in, w_out):
  t = x2.shape[0]
  row = pl.BlockSpec((FFN_ROWS, D_MODEL), lambda i: (i, 0))
  return pl.pallas_call(
      _ffn_body,
      grid=(t // FFN_ROWS,),
      in_specs=[row, _resident((1, D_MODEL)), _resident((1, D_MODEL)),
                _resident((D_MODEL, 2 * D_FF)), _resident((D_FF, D_MODEL))],
      out_specs=row,
      out_shape=jax.ShapeDtypeStruct((t, D_MODEL), _F32),
      compiler_params=pltpu.CompilerParams(
          dimension_semantics=("arbitrary",), vmem_limit_bytes=VMEM_LIMIT_BYTES),
      name="ffn",
  )(x2, npre, npost, w_in, w_out)


def _chunk_masks(n):
  shift = int(math.log2(CHUNK))
  row = lax.broadcasted_iota(jnp.int32, (n, n), 0)
  col = lax.broadcasted_iota(jnp.int32, (n, n), 1)
  same = lax.shift_right_logical(row, shift) == lax.shift_right_logical(col, shift)
  return same & (col <= row), same


def _mix_in_body(x_ref, wcat_ref, wglu_ref, wsm_ref, wsmt_ref, smct_ref, dww_ref,
                 npre_ref, smc_ref, pw1b_ref,
                 proj_ref, cv_ref, bg_ref, bgt_ref,
                 glu_ext, h_scr):
  n = IN_ROWS
  t = pl.program_id(1)

  @pl.when(t == 0)
  def _():
    glu_ext[:, 0:GLU_HALO, :] = jnp.zeros((CNV_TILES, GLU_HALO, LANES), _F32)

  h_scr[...] = _rms(x_ref[...], npre_ref[...]).astype(_BF16)
  h = h_scr[...]

  glu = _dot(h, wglu_ref[...]) + pw1b_ref[...]
  hh = glu[:, :CNV_CH] * _sigmoid(glu[:, CNV_CH:])
  for c in range(CNV_TILES):
    glu_ext[c, GLU_HALO:GLU_HALO + n, :] = hh[:, c * LANES:(c + 1) * LANES]

  tiles_per_piece = PROJ_PIECE // LANES
  reach = SUBLANES * ((CNV_K - 1) // SUBLANES)

  def conv_tile(c, carry):
    for r in range(0, n, CONV_ROWS):
      acc = jnp.zeros((CONV_ROWS // SUBLANES, SUBLANES, LANES), _F32)
      for s in range(SUBLANES):
        start = GLU_HALO - (CNV_K - 1) + s + r
        window = glu_ext[c, start:start + CONV_ROWS + reach, :]
        for j in range(s, CNV_K, SUBLANES):
          part = _vregs(window[j - s:j - s + CONV_ROWS, :])
          acc = acc + dww_ref[c, j][None] * part
      cv_ref[c, r:r + CONV_ROWS, :] = acc.reshape(CONV_ROWS, LANES)
    glu_ext[c, 0:GLU_HALO, :] = glu_ext[c, n:n + GLU_HALO, :]
    return carry

  for c in range(wcat_ref.shape[1] // PROJ_PIECE):
    p = _dot(h, wcat_ref[:, c * PROJ_PIECE:(c + 1) * PROJ_PIECE])
    for i in range(tiles_per_piece):
      proj_ref[c * tiles_per_piece + i] = p[:, i * LANES:(i + 1) * LANES]

  lax.fori_loop(0, CNV_TILES, conv_tile, 0)

  m = MIX_ROWS
  causal, same = _chunk_masks(m)
  upper = lax.broadcasted_iota(jnp.int32, (m, m), 0) <= lax.broadcasted_iota(jnp.int32, (m, m), 1)
  upper = (upper & same).astype(_BF16)
  lower_block = jnp.concatenate([causal.astype(_BF16), same.astype(_BF16)], axis=0)
  lane = lax.broadcasted_iota(jnp.int32, (m, SMALL_COLS), 1)
  sub = lax.broadcasted_iota(jnp.int32, (SMALL_ROWS, m), 0)

  def split3(v, axis):
    hi = v.astype(_BF16)
    rest = v - hi.astype(_F32)
    mid = rest.astype(_BF16)
    lo = (rest - mid.astype(_F32)).astype(_BF16)
    return jnp.concatenate([hi, mid, lo], axis=axis)

  for r in range(0, n, m):
    hr = h_scr[r:r + m, :]
    sm = _dot(hr, wsm_ref[...])
    g = -jnp.exp(smc_ref[0:1, :]) * _softplus(sm + smc_ref[1:2, :])
    sums = _dot(lower_block, split3(g, 1))
    sums = sums[:, :SMALL_COLS] + sums[:, SMALL_COLS:2 * SMALL_COLS] + sums[:, 2 * SMALL_COLS:]
    bg_ref[r:r + m, :] = jnp.where(lane < HEADS, _sigmoid(sm),
                                   jnp.where(lane < 2 * HEADS, sums[:m], sums[m:]))
    smt = _dot_nt(wsmt_ref[...], hr)
    gt = -jnp.exp(smct_ref[:, 0:1]) * _softplus(smt + smct_ref[:, 1:2])
    sums_t = _dot(split3(gt, 0), upper)
    sums_t = sums_t[:SMALL_ROWS] + sums_t[SMALL_ROWS:2 * SMALL_ROWS] + sums_t[2 * SMALL_ROWS:]
    bgt_ref[:, r:r + m] = jnp.where(sub < HEADS, _sigmoid(smt), sums_t)


def _mix_in(x2, npre, wcat, wglu, wsm, wsmt, smc, smct, pw1b, dww, batch, seq):
  t = x2.shape[0]
  nt = seq // IN_ROWS
  row = lambda width: pl.BlockSpec((IN_ROWS, width), lambda b, i: (b * nt + i, 0))
  tiles = lambda count: pl.BlockSpec((count, IN_ROWS, LANES), lambda b, i: (0, b * nt + i, 0))
  weights = [wcat, wglu, wsm, wsmt, smct, dww, npre, smc, pw1b]
  out_shape = (
      jax.ShapeDtypeStruct((4 * HEADS, t, LANES), _F32),
      jax.ShapeDtypeStruct((CNV_TILES, t, LANES), _F32),
      jax.ShapeDtypeStruct((t, SMALL_COLS), _F32),
      jax.ShapeDtypeStruct((SMALL_ROWS, t), _F32),
  )
  out_specs = (
      tiles(4 * HEADS), tiles(CNV_TILES), row(SMALL_COLS),
      pl.BlockSpec((SMALL_ROWS, IN_ROWS), lambda b, i: (0, b * nt + i)),
  )
  return pl.pallas_call(
      _mix_in_body,
      grid=(batch, nt),
      in_specs=[row(D_MODEL)] + [_resident(w.shape) for w in weights],
      out_specs=out_specs,
      out_shape=out_shape,
      scratch_shapes=[
          pltpu.VMEM((CNV_TILES, GLU_HALO + IN_ROWS, LANES), _F32),
          pltpu.VMEM((IN_ROWS, D_MODEL), _BF16),
      ],
      compiler_params=pltpu.CompilerParams(
          dimension_semantics=("arbitrary", "arbitrary"), vmem_limit_bytes=VMEM_LIMIT_BYTES),
      name="mix_in",
  )(x2, *weights)


def _gdn_body(q_ref, k_ref, v_ref, z_ref, bg_ref, bgt_ref, nw_ref, convw_ref, o_ref, state, ext):
  n = MIX_ROWS
  t = pl.program_id(1)
  heads = range(HEADS)

  @pl.when(t == 0)
  def _():
    state[...] = jnp.zeros((HEADS, HEAD_DIM, HEAD_DIM), _F32)
    ext[:, 0:QKV_HALO, :] = jnp.zeros((3 * HEADS, QKV_HALO, LANES), _F32)

  def conv(part, ref, j):
    c = part * HEADS + j
    ext[c, QKV_HALO:QKV_HALO + n, :] = ref[j]
    acc = jnp.zeros((n // SUBLANES, SUBLANES, LANES), _F32)
    for tap in range(GDN_CONV):
      start = QKV_HALO - (GDN_CONV - 1) + tap
      acc = acc + convw_ref[c, tap][None] * _vregs(ext[c, start:start + n, :])
    ext[c, 0:QKV_HALO, :] = ext[c, n:n + QKV_HALO, :]
    y = _silu(acc.reshape(n, LANES))
    if part < 2:
      y = y * lax.rsqrt(jnp.sum(y * y, axis=-1, keepdims=True) + 1e-6)
    if part == 0:
      y = y * (HEAD_DIM ** -0.5)
    return y

  bg = bg_ref[...]
  causal, _ = _chunk_masks(n)
  row = lax.broadcasted_iota(jnp.int32, (n, n), 0)
  col = lax.broadcasted_iota(jnp.int32, (n, n), 1)
  strict = causal & (col < row)
  eye = (row == col).astype(_F32)

  beta = [bg[:, j:j + 1] for j in heads]
  g_col = [bg[:, HEADS + j:HEADS + j + 1] for j in heads]
  g_last = [bg[:, 2 * HEADS + j:2 * HEADS + j + 1] for j in heads]
  qs, ks, vs, qkb, xs, ps = {}, {}, {}, {}, {}, {}
  us, ws, q_dec, k_dec = {}, {}, {}, {}

  def tile_stage(group):
    for j in group:
      q = conv(0, q_ref, j)
      k = conv(1, k_ref, j)
      v = conv(2, v_ref, j)
      g_row = bgt_ref[HEADS + j:HEADS + j + 1, :]
      d = jnp.where(causal, jnp.exp(jnp.where(causal, g_col[j] - g_row, 0.0)), 0.0)
      kb = k.astype(_BF16)
      qk = (_dot_nt(q.astype(_BF16), kb) * d).astype(_BF16)
      qkb[j] = [qk[c * CHUNK:(c + 1) * CHUNK, c * CHUNK:(c + 1) * CHUNK]
                for c in range(n // CHUNK)]
      a_mat = jnp.where(strict, _dot_nt(kb, kb) * d * beta[j], 0.0)
      xs[j] = eye - a_mat
      ps[j] = a_mat
      qs[j], ks[j], vs[j] = q, k, v

    for _ in range(int(math.log2(CHUNK)) - 1):
      for j in group:
        pb = ps[j].astype(_BF16)
        ps[j] = _dot(pb, pb)
        xs[j] = xs[j] + _dot(xs[j].astype(_BF16), ps[j].astype(_BF16))

    for j in group:
      e_g = jnp.exp(g_col[j])
      rhs = jnp.concatenate([vs[j] * beta[j], ks[j] * (beta[j] * e_g)], axis=-1).astype(_BF16)
      uw = _dot(xs[j].astype(_BF16), rhs)
      us[j] = uw[:, :HEAD_DIM]
      ws[j] = uw[:, HEAD_DIM:]
      q_dec[j] = qs[j] * e_g
      k_dec[j] = (ks[j] * jnp.exp(g_last[j] - g_col[j])).astype(_BF16)

  for first in range(0, HEADS, GDN_BATCH):
    tile_stage(range(first, first + GDN_BATCH))

  s = [state[j] for j in heads]
  for c in range(n // CHUNK):
    r = slice(c * CHUNK, (c + 1) * CHUNK)
    ws_qs = []
    for j in heads:
      wq = jnp.concatenate([ws[j][r], q_dec[j][r]], axis=0).astype(_BF16)
      ws_qs.append(_dot(wq, s[j].astype(_BF16)))
    for j in heads:
      v_new = (us[j][r] - ws_qs[j][:CHUNK]).astype(_BF16)
      o_c = ws_qs[j][CHUNK:] + _dot(qkb[j][c], v_new)
      d_c = jnp.exp(g_last[j][c * CHUNK:c * CHUNK + 1, :])
      s[j] = s[j] * d_c + _dot_tn(k_dec[j][r], v_new)
      o_n = _rms(o_c, nw_ref[...]) * _silu(z_ref[j, r, :])
      o_ref[r, j * HEAD_DIM:(j + 1) * HEAD_DIM] = o_n.astype(o_ref.dtype)
  for j in heads:
    state[j] = s[j]


def _gdn(proj, bg, bgt, norm_w, convw, batch, seq):
  t = bg.shape[0]
  nt = seq // MIX_ROWS

  def head_major(part):
    return pl.BlockSpec((HEADS, MIX_ROWS, LANES), lambda b, i: (part, b * nt + i, 0))

  return pl.pallas_call(
      _gdn_body,
      grid=(batch, nt),
      in_specs=[head_major(0), head_major(1), head_major(2), head_major(3),
                pl.BlockSpec((MIX_ROWS, SMALL_COLS), lambda b, i: (b * nt + i, 0)),
                pl.BlockSpec((SMALL_ROWS, MIX_ROWS), lambda b, i: (0, b * nt + i)),
                _resident((1, HEAD_DIM)), _resident(convw.shape)],
      out_specs=pl.BlockSpec((MIX_ROWS, D_MODEL), lambda b, i: (b * nt + i, 0)),
      out_shape=jax.ShapeDtypeStruct((t, D_MODEL), _BF16),
      scratch_shapes=[pltpu.VMEM((HEADS, HEAD_DIM, HEAD_DIM), _F32),
                      pltpu.VMEM((3 * HEADS, QKV_HALO + MIX_ROWS, LANES), _F32)],
      compiler_params=pltpu.CompilerParams(
          dimension_semantics=("arbitrary", "arbitrary"), vmem_limit_bytes=VMEM_LIMIT_BYTES),
      name="gdn",
  )(proj, proj, proj, proj, bg, bgt, norm_w, convw)


def _mix_out_body(x_ref, og_ref, cv_ref, npre_ref, wgat_ref, wgo_ref, wmo_ref, cwo_ref, dwb_ref,
                  lng_ref, lnb_ref, cbo_ref, npost_ref, o_ref):
  x = x_ref[...]
  gates = _sigmoid(_dot(_rms(x, npre_ref[...]).astype(_BF16), wgat_ref[...]))
  cv = jnp.concatenate([cv_ref[c] for c in range(CNV_TILES)], axis=-1) + dwb_ref[...]
  mu = jnp.mean(cv, axis=-1, keepdims=True)
  cen = cv - mu
  var = jnp.mean(cen * cen, axis=-1, keepdims=True)
  ln = cen * lax.rsqrt(var + LN_EPS) * lng_ref[...] + lnb_ref[...]
  y_b = _dot(_silu(ln).astype(_BF16), cwo_ref[...]) + cbo_ref[...]
  y_a = _dot(og_ref[...], wgo_ref[...])
  y = gates[:, :D_MODEL] * y_a + gates[:, D_MODEL:] * y_b
  y2 = _dot(y.astype(_BF16), wmo_ref[...])
  o_ref[...] = x + _rms(y2, npost_ref[...])


def _mix_out(x2, og, cv, npre, wgat, wgo, wmo, cwo, dwb, lng, lnb, cbo, npost):
  t = x2.shape[0]
  row = pl.BlockSpec((OUT_ROWS, D_MODEL), lambda i: (i, 0))
  square = _resident((D_MODEL, D_MODEL))
  vec = _resident((1, D_MODEL))
  return pl.pallas_call(
      _mix_out_body,
      grid=(t // OUT_ROWS,),
      in_specs=[row, row, pl.BlockSpec((CNV_TILES, OUT_ROWS, LANES), lambda i: (0, i, 0)),
                vec, _resident((D_MODEL, 2 * D_MODEL)), square, square, square,
                vec, vec, vec, vec, vec],
      out_specs=row,
      out_shape=jax.ShapeDtypeStruct((t, D_MODEL), _F32),
      compiler_params=pltpu.CompilerParams(
          dimension_semantics=("arbitrary",), vmem_limit_bytes=VMEM_LIMIT_BYTES),
      name="mix_out",
  )(x2, og, cv, npre, wgat, wgo, wmo, cwo, dwb, lng, lnb, cbo, npost)


def _row(v):
  return v.reshape(1, -1).astype(_F32)


def _lane_tiles(w):
  k, c = w.shape
  tiles = w.reshape(k, c // LANES, LANES).transpose(1, 0, 2).astype(_F32)
  return jnp.broadcast_to(tiles[:, :, None, :], (c // LANES, k, SUBLANES, LANES))


def kernel(x, ffn1_norm_pre, ffn1_norm_post, ffn1_w_in, ffn1_w_out, mix_norm_pre, mix_norm_post,
           mix_w_in, gdn_conv_w, gdn_a_log, gdn_dt_bias, gdn_norm_w, gdn_w_o, cnv_pw1_b, cnv_dw_w,
           cnv_dw_b, cnv_ln_g, cnv_ln_b, cnv_w_o, cnv_b_o, mix_w_out, ffn2_norm_pre,
           ffn2_norm_post, ffn2_w_in, ffn2_w_out):
  batch, seq, _ = x.shape
  depth = ffn1_w_in.shape[0]
  x2 = x.reshape(batch * seq, D_MODEL)
  o_qkv = 3 * HEADS * HEAD_DIM
  o_z = o_qkv + HEADS * HEAD_DIM
  o_b = o_z + HEADS
  o_a = o_b + HEADS
  o_glu = o_a + 2 * CNV_CH
  for i in range(depth):
    x2 = _ffn(x2, _row(ffn1_norm_pre[i]), _row(ffn1_norm_post[i]),
              ffn1_w_in[i].astype(_BF16), ffn1_w_out[i].astype(_BF16))

    w = mix_w_in[i]
    w_beta = w[:, o_z:o_b]
    w_a = w[:, o_b:o_a]
    wsm = jnp.concatenate(
        [w_beta, w_a, w_a, jnp.zeros((D_MODEL, SMALL_COLS - 3 * HEADS), _F32)], axis=1)
    wsmt = jnp.concatenate(
        [w_beta, w_a, w_a, jnp.zeros((D_MODEL, SMALL_ROWS - 3 * HEADS), _F32)], axis=1).T
    pad = jnp.zeros((HEADS,), _F32)
    a_log = jnp.concatenate([pad, gdn_a_log[i], gdn_a_log[i]])
    dt_b = jnp.concatenate([pad, gdn_dt_bias[i], gdn_dt_bias[i]])
    smc = jnp.zeros((SUBLANES, SMALL_COLS), _F32)
    smc = smc.at[0, :3 * HEADS].set(a_log).at[1, :3 * HEADS].set(dt_b)
    smct = jnp.zeros((SMALL_ROWS, LANES), _F32)
    smct = smct.at[:3 * HEADS, 0].set(a_log).at[:3 * HEADS, 1].set(dt_b)

    npre = _row(mix_norm_pre[i])
    proj, cv, bg, bgt = _mix_in(
        x2, npre, w[:, :o_z].astype(_BF16), w[:, o_a:o_glu].astype(_BF16),
        wsm.astype(_BF16), wsmt.astype(_BF16), smc, smct, _row(cnv_pw1_b[i]),
        _lane_tiles(cnv_dw_w[i]), batch, seq)
    og = _gdn(proj, bg, bgt, _row(gdn_norm_w[i]), _lane_tiles(gdn_conv_w[i]), batch, seq)
    x2 = _mix_out(x2, og, cv, npre, w[:, o_glu:].astype(_BF16), gdn_w_o[i].astype(_BF16),
                  mix_w_out[i].astype(_BF16), cnv_w_o[i].astype(_BF16), _row(cnv_dw_b[i]),
                  _row(cnv_ln_g[i]), _row(cnv_ln_b[i]), _row(cnv_b_o[i]), _row(mix_norm_post[i]))

    x2 = _ffn(x2, _row(ffn2_norm_pre[i]), _row(ffn2_norm_post[i]),
              ffn2_w_in[i].astype(_BF16), ffn2_w_out[i].astype(_BF16))
  return x2.reshape(batch, seq, D_MODEL)
```

```python
import math

import jax
import jax.numpy as jnp
from jax import lax
from jax.experimental import pallas as pl
from jax.experimental.pallas import tpu as pltpu

D_MODEL = 1024
D_FF = 2816
HEADS = 8
HEAD_DIM = 128
GDN_CONV = 4
CNV_CH = 1024
CNV_K = 31
RMS_EPS = 1e-6
LN_EPS = 1e-5

LANES = 128
SUBLANES = 8
VMEM_LIMIT_BYTES = 56 * 1024 * 1024

FFN_ROWS = 512
FF_CHUNK = 2816
IN_ROWS = 512
MIX_ROWS = 256
PROJ_PIECE = 512
CONV_ROWS = 128
OUT_ROWS = 512
CHUNK = 64
GDN_BATCH = 4
QKV_HALO = SUBLANES
GLU_HALO = 4 * SUBLANES
SMALL_COLS = LANES
SMALL_ROWS = 4 * SUBLANES
CNV_TILES = CNV_CH // LANES

_BF16 = jnp.bfloat16
_F32 = jnp.float32


def _dot(a, b):
  return jnp.dot(a, b, preferred_element_type=_F32)


def _dot_nt(a, b):
  return lax.dot_general(a, b, (((1,), (1,)), ((), ())), preferred_element_type=_F32)


def _dot_tn(a, b):
  return lax.dot_general(a, b, (((0,), (0,)), ((), ())), preferred_element_type=_F32)


def _rms(x, w):
  ms = jnp.mean(x * x, axis=-1, keepdims=True)
  return x * lax.rsqrt(ms + RMS_EPS) * w


def _sigmoid(x):
  return 0.5 * jnp.tanh(0.5 * x) + 0.5


def _silu(x):
  half = 0.5 * x
  return half * jnp.tanh(half) + half


def _softplus(x):
  return jnp.maximum(x, 0.0) + jnp.log1p(jnp.exp(-jnp.abs(x)))


def _resident(shape):
  zeros = (0,) * len(shape)
  return pl.BlockSpec(shape, lambda *_: zeros, pipeline_mode=pl.Buffered(1))


def _vregs(x):
  return x.reshape(x.shape[0] // SUBLANES, SUBLANES, LANES)


def _ffn_body(x_ref, npre_ref, npost_ref, win_ref, wout_ref, o_ref):
  x = x_ref[...]
  h = _rms(x, npre_ref[...]).astype(_BF16)
  acc = jnp.zeros((FFN_ROWS, D_MODEL), _F32)
  for c in range(D_FF // FF_CHUNK):
    lo = c * FF_CHUNK
    gate = _dot(h, win_ref[:, lo:lo + FF_CHUNK])
    up = _dot(h, win_ref[:, D_FF + lo:D_FF + lo + FF_CHUNK])
    act = (_silu(gate) * up).astype(_BF16)
    acc = acc + _dot(act, wout_ref[lo:lo + FF_CHUNK, :])
  o_ref[...] = x + 0.5 * _rms(acc, npost_ref[...])


def _ffn(x2, npre, npost, w_in, w_out):
  t = x2.shape[0]
  row = pl.BlockSpec((FFN_ROWS, D_MODEL), lambda i: (i, 0))
  return pl.pallas_call(
      _ffn_body,
      grid=(t // FFN_ROWS,),
      in_specs=[row, _resident((1, D_MODEL)), _resident((1, D_MODEL)),
                _resident((D_MODEL, 2 * D_FF)), _resident((D_FF, D_MODEL))],
      out_specs=row,
      out_shape=jax.ShapeDtypeStruct((t, D_MODEL), _F32),
      compiler_params=pltpu.CompilerParams(
          dimension_semantics=("arbitrary",), vmem_limit_bytes=VMEM_LIMIT_BYTES),
      name="ffn",
  )(x2, npre, npost, w_in, w_out)


def _chunk_masks(n):
  shift = int(math.log2(CHUNK))
  row = lax.broadcasted_iota(jnp.int32, (n, n), 0)
  col = lax.broadcasted_iota(jnp.int32, (n, n), 1)
  same = lax.shift_right_logical(row, shift) == lax.shift_right_logical(col, shift)
  return same & (col <= row), same


def _mix_in_body(x_ref, wcat_ref, wglu_ref, wsm_ref, wsmt_ref, smct_ref, dww_ref,
                 npre_ref, smc_ref, pw1b_ref,
                 proj_ref, cv_ref, bg_ref, bgt_ref,
                 glu_ext, h_scr):
  n = IN_ROWS
  t = pl.program_id(1)

  @pl.when(t == 0)
  def _():
    glu_ext[:, 0:GLU_HALO, :] = jnp.zeros((CNV_TILES, GLU_HALO, LANES), _F32)

  h_scr[...] = _rms(x_ref[...], npre_ref[...]).astype(_BF16)
  h = h_scr[...]

  glu = _dot(h, wglu_ref[...]) + pw1b_ref[...]
  hh = glu[:, :CNV_CH] * _sigmoid(glu[:, CNV_CH:])
  for c in range(CNV_TILES):
    glu_ext[c, GLU_HALO:GLU_HALO + n, :] = hh[:, c * LANES:(c + 1) * LANES]

  tiles_per_piece = PROJ_PIECE // LANES
  reach = SUBLANES * ((CNV_K - 1) // SUBLANES)

  def conv_tile(c, carry):
    for r in range(0, n, CONV_ROWS):
      acc = jnp.zeros((CONV_ROWS // SUBLANES, SUBLANES, LANES), _F32)
      for s in range(SUBLANES):
        start = GLU_HALO - (CNV_K - 1) + s + r
        window = glu_ext[c, start:start + CONV_ROWS + reach, :]
        for j in range(s, CNV_K, SUBLANES):
          part = _vregs(window[j - s:j - s + CONV_ROWS, :])
          acc = acc + dww_ref[c, j][None] * part
      cv_ref[c, r:r + CONV_ROWS, :] = acc.reshape(CONV_ROWS, LANES)
    glu_ext[c, 0:GLU_HALO, :] = glu_ext[c, n:n + GLU_HALO, :]
    return carry

  for c in range(wcat_ref.shape[1] // PROJ_PIECE):
    p = _dot(h, wcat_ref[:, c * PROJ_PIECE:(c + 1) * PROJ_PIECE])
    for i in range(tiles_per_piece):
      proj_ref[c * tiles_per_piece + i] = p[:, i * LANES:(i + 1) * LANES]

  lax.fori_loop(0, CNV_TILES, conv_tile, 0)

  m = MIX_ROWS
  causal, same = _chunk_masks(m)
  upper = lax.broadcasted_iota(jnp.int32, (m, m), 0) <= lax.broadcasted_iota(jnp.int32, (m, m), 1)
  upper = (upper & same).astype(_BF16)
  lower_block = jnp.concatenate([causal.astype(_BF16), same.astype(_BF16)], axis=0)
  lane = lax.broadcasted_iota(jnp.int32, (m, SMALL_COLS), 1)
  sub = lax.broadcasted_iota(jnp.int32, (SMALL_ROWS, m), 0)

  def split3(v, axis):
    hi = v.astype(_BF16)
    rest = v - hi.astype(_F32)
    mid = rest.astype(_BF16)
    lo = (rest - mid.astype(_F32)).astype(_BF16)
    return jnp.concatenate([hi, mid, lo], axis=axis)

  for r in range(0, n, m):
    hr = h_scr[r:r + m, :]
    sm = _dot(hr, wsm_ref[...])
    g = -jnp.exp(smc_ref[0:1, :]) * _softplus(sm + smc_ref[1:2, :])
    sums = _dot(lower_block, split3(g, 1))
    sums = sums[:, :SMALL_COLS] + sums[:, SMALL_COLS:2 * SMALL_COLS] + sums[:, 2 * SMALL_COLS:]
    bg_ref[r:r + m, :] = jnp.where(lane < HEADS, _sigmoid(sm),
                                   jnp.where(lane < 2 * HEADS, sums[:m], sums[m:]))
    smt = _dot_nt(wsmt_ref[...], hr)
    gt = -jnp.exp(smct_ref[:, 0:1]) * _softplus(smt + smct_ref[:, 1:2])
    sums_t = _dot(split3(gt, 0), upper)
    sums_t = sums_t[:SMALL_ROWS] + sums_t[SMALL_ROWS:2 * SMALL_ROWS] + sums_t[2 * SMALL_ROWS:]
    bgt_ref[:, r:r + m] = jnp.where(sub < HEADS, _sigmoid(smt), sums_t)


def _mix_in(x2, npre, wcat, wglu, wsm, wsmt, smc, smct, pw1b, dww, batch, seq):
  t = x2.shape[0]
  nt = seq // IN_ROWS
  row = lambda width: pl.BlockSpec((IN_ROWS, width), lambda b, i: (b * nt + i, 0))
  tiles = lambda count: pl.BlockSpec((count, IN_ROWS, LANES), lambda b, i: (0, b * nt + i, 0))
  weights = [wcat, wglu, wsm, wsmt, smct, dww, npre, smc, pw1b]
  out_shape = (
      jax.ShapeDtypeStruct((4 * HEADS, t, LANES), _F32),
      jax.ShapeDtypeStruct((CNV_TILES, t, LANES), _F32),
      jax.ShapeDtypeStruct((t, SMALL_COLS), _F32),
      jax.ShapeDtypeStruct((SMALL_ROWS, t), _F32),
  )
  out_specs = (
      tiles(4 * HEADS), tiles(CNV_TILES), row(SMALL_COLS),
      pl.BlockSpec((SMALL_ROWS, IN_ROWS), lambda b, i: (0, b * nt + i)),
  )
  return pl.pallas_call(
      _mix_in_body,
      grid=(batch, nt),
      in_specs=[row(D_MODEL)] + [_resident(w.shape) for w in weights],
      out_specs=out_specs,
      out_shape=out_shape,
      scratch_shapes=[
          pltpu.VMEM((CNV_TILES, GLU_HALO + IN_ROWS, LANES), _F32),
          pltpu.VMEM((IN_ROWS, D_MODEL), _BF16),
      ],
      compiler_params=pltpu.CompilerParams(
          dimension_semantics=("arbitrary", "arbitrary"), vmem_limit_bytes=VMEM_LIMIT_BYTES),
      name="mix_in",
  )(x2, *weights)


def _gdn_body(q_ref, k_ref, v_ref, z_ref, bg_ref, bgt_ref, nw_ref, convw_ref, o_ref, state, ext):
  n = MIX_ROWS
  t = pl.program_id(1)
  heads = range(HEADS)

  @pl.when(t == 0)
  def _():
    state[...] = jnp.zeros((HEADS, HEAD_DIM, HEAD_DIM), _F32)
    ext[:, 0:QKV_HALO, :] = jnp.zeros((3 * HEADS, QKV_HALO, LANES), _F32)

  def conv(part, ref, j):
    c = part * HEADS + j
    ext[c, QKV_HALO:QKV_HALO + n, :] = ref[j]
    acc = jnp.zeros((n // SUBLANES, SUBLANES, LANES), _F32)
    for tap in range(GDN_CONV):
      start = QKV_HALO - (GDN_CONV - 1) + tap
      acc = acc + convw_ref[c, tap][None] * _vregs(ext[c, start:start + n, :])
    ext[c, 0:QKV_HALO, :] = ext[c, n:n + QKV_HALO, :]
    y = _silu(acc.reshape(n, LANES))
    if part < 2:
      y = y * lax.rsqrt(jnp.sum(y * y, axis=-1, keepdims=True) + 1e-6)
    if part == 0:
      y = y * (HEAD_DIM ** -0.5)
    return y

  bg = bg_ref[...]
  causal, _ = _chunk_masks(n)
  row = lax.broadcasted_iota(jnp.int32, (n, n), 0)
  col = lax.broadcasted_iota(jnp.int32, (n, n), 1)
  strict = causal & (col < row)
  eye = (row == col).astype(_F32)

  beta = [bg[:, j:j + 1] for j in heads]
  g_col = [bg[:, HEADS + j:HEADS + j + 1] for j in heads]
  g_last = [bg[:, 2 * HEADS + j:2 * HEADS + j + 1] for j in heads]
  qs, ks, vs, qkb, xs, ps = {}, {}, {}, {}, {}, {}
  us, ws, q_dec, k_dec = {}, {}, {}, {}

  def tile_stage(group):
    for j in group:
      q = conv(0, q_ref, j)
      k = conv(1, k_ref, j)
      v = conv(2, v_ref, j)
      g_row = bgt_ref[HEADS + j:HEADS + j + 1, :]
      d = jnp.where(causal, jnp.exp(jnp.where(causal, g_col[j] - g_row, 0.0)), 0.0)
      kb = k.astype(_BF16)
      qk = (_dot_nt(q.astype(_BF16), kb) * d).astype(_BF16)
      qkb[j] = [qk[c * CHUNK:(c + 1) * CHUNK, c * CHUNK:(c + 1) * CHUNK]
                for c in range(n // CHUNK)]
      a_mat = jnp.where(strict, _dot_nt(kb, kb) * d * beta[j], 0.0)
      xs[j] = eye - a_mat
      ps[j] = a_mat
      qs[j], ks[j], vs[j] = q, k, v

    for _ in range(int(math.log2(CHUNK)) - 1):
      for j in group:
        pb = ps[j].astype(_BF16)
        ps[j] = _dot(pb, pb)
        xs[j] = xs[j] + _dot(xs[j].astype(_BF16), ps[j].astype(_BF16))

    for j in group:
      e_g = jnp.exp(g_col[j])
      rhs = jnp.concatenate([vs[j] * beta[j], ks[j] * (beta[j] * e_g)], axis=-1).astype(_BF16)
      uw = _dot(xs[j].astype(_BF16), rhs)
      us[j] = uw[:, :HEAD_DIM]
      ws[j] = uw[:, HEAD_DIM:]
      q_dec[j] = qs[j] * e_g
      k_dec[j] = (ks[j] * jnp.exp(g_last[j] - g_col[j])).astype(_BF16)

  for first in range(0, HEADS, GDN_BATCH):
    tile_stage(range(first, first + GDN_BATCH))

  s = [state[j] for j in heads]
  for c in range(n // CHUNK):
    r = slice(c * CHUNK, (c + 1) * CHUNK)
    ws_qs = []
    for j in heads:
      wq = jnp.concatenate([ws[j][r], q_dec[j][r]], axis=0).astype(_BF16)
      ws_qs.append(_dot(wq, s[j].astype(_BF16)))
    for j in heads:
      v_new = (us[j][r] - ws_qs[j][:CHUNK]).astype(_BF16)
      o_c = ws_qs[j][CHUNK:] + _dot(qkb[j][c], v_new)
      d_c = jnp.exp(g_last[j][c * CHUNK:c * CHUNK + 1, :])
      s[j] = s[j] * d_c + _dot_tn(k_dec[j][r], v_new)
      o_n = _rms(o_c, nw_ref[...]) * _silu(z_ref[j, r, :])
      o_ref[r, j * HEAD_DIM:(j + 1) * HEAD_DIM] = o_n.astype(o_ref.dtype)
  for j in heads:
    state[j] = s[j]


def _gdn(proj, bg, bgt, norm_w, convw, batch, seq):
  t = bg.shape[0]
  nt = seq // MIX_ROWS

  def head_major(part):
    return pl.BlockSpec((HEADS, MIX_ROWS, LANES), lambda b, i: (part, b * nt + i, 0))

  return pl.pallas_call(
      _gdn_body,
      grid=(batch, nt),
      in_specs=[head_major(0), head_major(1), head_major(2), head_major(3),
                pl.BlockSpec((MIX_ROWS, SMALL_COLS), lambda b, i: (b * nt + i, 0)),
                pl.BlockSpec((SMALL_ROWS, MIX_ROWS), lambda b, i: (0, b * nt + i)),
                _resident((1, HEAD_DIM)), _resident(convw.shape)],
      out_specs=pl.BlockSpec((MIX_ROWS, D_MODEL), lambda b, i: (b * nt + i, 0)),
      out_shape=jax.ShapeDtypeStruct((t, D_MODEL), _BF16),
      scratch_shapes=[pltpu.VMEM((HEADS, HEAD_DIM, HEAD_DIM), _F32),
                      pltpu.VMEM((3 * HEADS, QKV_HALO + MIX_ROWS, LANES), _F32)],
      compiler_params=pltpu.CompilerParams(
          dimension_semantics=("arbitrary", "arbitrary"), vmem_limit_bytes=VMEM_LIMIT_BYTES),
      name="gdn",
  )(proj, proj, proj, proj, bg, bgt, norm_w, convw)


def _mix_out_body(x_ref, og_ref, cv_ref, npre_ref, wgat_ref, wgo_ref, wmo_ref, cwo_ref, dwb_ref,
                  lng_ref, lnb_ref, cbo_ref, npost_ref, o_ref):
  x = x_ref[...]
  gates = _sigmoid(_dot(_rms(x, npre_ref[...]).astype(_BF16), wgat_ref[...]))
  cv = jnp.concatenate([cv_ref[c] for c in range(CNV_TILES)], axis=-1) + dwb_ref[...]
  mu = jnp.mean(cv, axis=-1, keepdims=True)
  cen = cv - mu
  var = jnp.mean(cen * cen, axis=-1, keepdims=True)
  ln = cen * lax.rsqrt(var + LN_EPS) * lng_ref[...] + lnb_ref[...]
  y_b = _dot(_silu(ln).astype(_BF16), cwo_ref[...]) + cbo_ref[...]
  y_a = _dot(og_ref[...], wgo_ref[...])
  y = gates[:, :D_MODEL] * y_a + gates[:, D_MODEL:] * y_b
  y2 = _dot(y.astype(_BF16), wmo_ref[...])
  o_ref[...] = x + _rms(y2, npost_ref[...])


def _mix_out(x2, og, cv, npre, wgat, wgo, wmo, cwo, dwb, lng, lnb, cbo, npost):
  t = x2.shape[0]
  row = pl.BlockSpec((OUT_ROWS, D_MODEL), lambda i: (i, 0))
  square = _resident((D_MODEL, D_MODEL))
  vec = _resident((1, D_MODEL))
  return pl.pallas_call(
      _mix_out_body,
      grid=(t // OUT_ROWS,),
      in_specs=[row, row, pl.BlockSpec((CNV_TILES, OUT_ROWS, LANES), lambda i: (0, i, 0)),
                vec, _resident((D_MODEL, 2 * D_MODEL)), square, square, square,
                vec, vec, vec, vec, vec],
      out_specs=row,
      out_shape=jax.ShapeDtypeStruct((t, D_MODEL), _F32),
      compiler_params=pltpu.CompilerParams(
          dimension_semantics=("arbitrary",), vmem_limit_bytes=VMEM_LIMIT_BYTES),
      name="mix_out",
  )(x2, og, cv, npre, wgat, wgo, wmo, cwo, dwb, lng, lnb, cbo, npost)


def _row(v):
  return v.reshape(1, -1).astype(_F32)


def _lane_tiles(w):
  k, c = w.shape
  tiles = w.reshape(k, c // LANES, LANES).transpose(1, 0, 2).astype(_F32)
  return jnp.broadcast_to(tiles[:, :, None, :], (c // LANES, k, SUBLANES, LANES))


def kernel(x, ffn1_norm_pre, ffn1_norm_post, ffn1_w_in, ffn1_w_out, mix_norm_pre, mix_norm_post,
           mix_w_in, gdn_conv_w, gdn_a_log, gdn_dt_bias, gdn_norm_w, gdn_w_o, cnv_pw1_b, cnv_dw_w,
           cnv_dw_b, cnv_ln_g, cnv_ln_b, cnv_w_o, cnv_b_o, mix_w_out, ffn2_norm_pre,
           ffn2_norm_post, ffn2_w_in, ffn2_w_out):
  batch, seq, _ = x.shape
  depth = ffn1_w_in.shape[0]
  x2 = x.reshape(batch * seq, D_MODEL)
  o_qkv = 3 * HEADS * HEAD_DIM
  o_z = o_qkv + HEADS * HEAD_DIM
  o_b = o_z + HEADS
  o_a = o_b + HEADS
  o_glu = o_a + 2 * CNV_CH
  for i in range(depth):
    x2 = _ffn(x2, _row(ffn1_norm_pre[i]), _row(ffn1_norm_post[i]),
              ffn1_w_in[i].astype(_BF16), ffn1_w_out[i].astype(_BF16))

    w = mix_w_in[i]
    w_beta = w[:, o_z:o_b]
    w_a = w[:, o_b:o_a]
    wsm = jnp.concatenate(
        [w_beta, w_a, w_a, jnp.zeros((D_MODEL, SMALL_COLS - 3 * HEADS), _F32)], axis=1)
    wsmt = jnp.concatenate(
        [w_beta, w_a, w_a, jnp.zeros((D_MODEL, SMALL_ROWS - 3 * HEADS), _F32)], axis=1).T
    pad = jnp.zeros((HEADS,), _F32)
    a_log = jnp.concatenate([pad, gdn_a_log[i], gdn_a_log[i]])
    dt_b = jnp.concatenate([pad, gdn_dt_bias[i], gdn_dt_bias[i]])
    smc = jnp.zeros((SUBLANES, SMALL_COLS), _F32)
    smc = smc.at[0, :3 * HEADS].set(a_log).at[1, :3 * HEADS].set(dt_b)
    smct = jnp.zeros((SMALL_ROWS, LANES), _F32)
    smct = smct.at[:3 * HEADS, 0].set(a_log).at[:3 * HEADS, 1].set(dt_b)

    npre = _row(mix_norm_pre[i])
    proj, cv, bg, bgt = _mix_in(
        x2, npre, w[:, :o_z].astype(_BF16), w[:, o_a:o_glu].astype(_BF16),
        wsm.astype(_BF16), wsmt.astype(_BF16), smc, smct, _row(cnv_pw1_b[i]),
        _lane_tiles(cnv_dw_w[i]), batch, seq)
    og = _gdn(proj, bg, bgt, _row(gdn_norm_w[i]), _lane_tiles(gdn_conv_w[i]), batch, seq)
    x2 = _mix_out(x2, og, cv, npre, w[:, o_glu:].astype(_BF16), gdn_w_o[i].astype(_BF16),
                  mix_w_out[i].astype(_BF16), cnv_w_o[i].astype(_BF16), _row(cnv_dw_b[i]),
                  _row(cnv_ln_g[i]), _row(cnv_ln_b[i]), _row(cnv_b_o[i]), _row(mix_norm_post[i]))

    x2 = _ffn(x2, _row(ffn2_norm_pre[i]), _row(ffn2_norm_post[i]),
              ffn2_w_in[i].astype(_BF16), ffn2_w_out[i].astype(_BF16))
  return x2.reshape(batch, seq, D_MODEL)
```

```python
import math

import jax
import jax.numpy as jnp
from jax import lax
from jax.experimental import pallas as pl
from jax.experimental.pallas import tpu as pltpu

D_MODEL = 1024
D_FF = 2816
HEADS = 8
HEAD_DIM = 128
GDN_CONV = 4
CNV_CH = 1024
CNV_K = 31
RMS_EPS = 1e-6
LN_EPS = 1e-5

LANES = 128
SUBLANES = 8
VMEM_LIMIT_BYTES = 56 * 1024 * 1024

FFN_ROWS = 512
FF_CHUNK = 2816
IN_ROWS = 512
MIX_ROWS = 256
PROJ_PIECE = 512
CONV_ROWS = 128
OUT_ROWS = 512
CHUNK = 64
GDN_BATCH = 4
QKV_HALO = SUBLANES
GLU_HALO = 4 * SUBLANES
SMALL_COLS = LANES
SMALL_ROWS = 4 * SUBLANES
CNV_TILES = CNV_CH // LANES

_BF16 = jnp.bfloat16
_F32 = jnp.float32


def _dot(a, b):
  return jnp.dot(a, b, preferred_element_type=_F32)


def _dot_nt(a, b):
  return lax.dot_general(a, b, (((1,), (1,)), ((), ())), preferred_element_type=_F32)


def _dot_tn(a, b):
  return lax.dot_general(a, b, (((0,), (0,)), ((), ())), preferred_element_type=_F32)


def _rms(x, w):
  ms = jnp.mean(x * x, axis=-1, keepdims=True)
  return x * lax.rsqrt(ms + RMS_EPS) * w


def _sigmoid(x):
  return 0.5 * jnp.tanh(0.5 * x) + 0.5


def _silu(x):
  half = 0.5 * x
  return half * jnp.tanh(half) + half


def _softplus(x):
  return jnp.maximum(x, 0.0) + jnp.log1p(jnp.exp(-jnp.abs(x)))


def _resident(shape):
  zeros = (0,) * len(shape)
  return pl.BlockSpec(shape, lambda *_: zeros, pipeline_mode=pl.Buffered(1))


def _layer(layer, rows, cols):
  return pl.BlockSpec((None, rows, cols), lambda *_: (layer, 0, 0), pipeline_mode=pl.Buffered(1))


def _vregs(x):
  return x.reshape(x.shape[0] // SUBLANES, SUBLANES, LANES)


def _ffn_body(x_ref, npre_ref, npost_ref, win_ref, wout_ref, o_ref):
  x = x_ref[...]
  h = _rms(x, npre_ref[...]).astype(_BF16)
  acc = jnp.zeros((FFN_ROWS, D_MODEL), _F32)
  for c in range(D_FF // FF_CHUNK):
    lo = c * FF_CHUNK
    gate = _dot(h, win_ref[:, lo:lo + FF_CHUNK])
    up = _dot(h, win_ref[:, D_FF + lo:D_FF + lo + FF_CHUNK])
    act = (_silu(gate) * up).astype(_BF16)
    acc = acc + _dot(act, wout_ref[lo:lo + FF_CHUNK, :])
  o_ref[...] = x + 0.5 * _rms(acc, npost_ref[...])


def _ffn(x2, npre, npost, w_in, w_out, layer):
  t = x2.shape[0]
  row = pl.BlockSpec((FFN_ROWS, D_MODEL), lambda i: (i, 0))
  return pl.pallas_call(
      _ffn_body,
      grid=(t // FFN_ROWS,),
      in_specs=[row, _resident((1, D_MODEL)), _resident((1, D_MODEL)),
                _layer(layer, D_MODEL, 2 * D_FF), _layer(layer, D_FF, D_MODEL)],
      out_specs=row,
      out_shape=jax.ShapeDtypeStruct((t, D_MODEL), _F32),
      compiler_params=pltpu.CompilerParams(
          dimension_semantics=("arbitrary",), vmem_limit_bytes=VMEM_LIMIT_BYTES),
      name="ffn",
  )(x2, npre, npost, w_in, w_out)


def _chunk_masks(n):
  shift = int(math.log2(CHUNK))
  row = lax.broadcasted_iota(jnp.int32, (n, n), 0)
  col = lax.broadcasted_iota(jnp.int32, (n, n), 1)
  same = lax.shift_right_logical(row, shift) == lax.shift_right_logical(col, shift)
  return same & (col <= row), same


def _mix_in_body(x_ref, wcat_ref, wglu_ref, wsm_ref, wsmt_ref, smct_ref, dww_ref,
                 npre_ref, smc_ref, pw1b_ref,
                 proj_ref, cv_ref, bg_ref, bgt_ref,
                 glu_ext, h_scr):
  n = IN_ROWS
  t = pl.program_id(1)

  @pl.when(t == 0)
  def _():
    glu_ext[:, 0:GLU_HALO, :] = jnp.zeros((CNV_TILES, GLU_HALO, LANES), _F32)

  h_scr[...] = _rms(x_ref[...], npre_ref[...]).astype(_BF16)
  h = h_scr[...]

  glu = _dot(h, wglu_ref[...]) + pw1b_ref[...]
  hh = glu[:, :CNV_CH] * _sigmoid(glu[:, CNV_CH:])
  for c in range(CNV_TILES):
    glu_ext[c, GLU_HALO:GLU_HALO + n, :] = hh[:, c * LANES:(c + 1) * LANES]

  tiles_per_piece = PROJ_PIECE // LANES
  reach = SUBLANES * ((CNV_K - 1) // SUBLANES)

  def conv_tile(c, carry):
    for r in range(0, n, CONV_ROWS):
      acc = jnp.zeros((CONV_ROWS // SUBLANES, SUBLANES, LANES), _F32)
      for s in range(SUBLANES):
        start = GLU_HALO - (CNV_K - 1) + s + r
        window = glu_ext[c, start:start + CONV_ROWS + reach, :]
        for j in range(s, CNV_K, SUBLANES):
          part = _vregs(window[j - s:j - s + CONV_ROWS, :])
          acc = acc + dww_ref[c, j][None] * part
      cv_ref[c, r:r + CONV_ROWS, :] = acc.reshape(CONV_ROWS, LANES)
    glu_ext[c, 0:GLU_HALO, :] = glu_ext[c, n:n + GLU_HALO, :]
    return carry

  for c in range(wcat_ref.shape[1] // PROJ_PIECE):
    p = _dot(h, wcat_ref[:, c * PROJ_PIECE:(c + 1) * PROJ_PIECE])
    for i in range(tiles_per_piece):
      proj_ref[c * tiles_per_piece + i] = p[:, i * LANES:(i + 1) * LANES]

  lax.fori_loop(0, CNV_TILES, conv_tile, 0)

  m = MIX_ROWS
  causal, same = _chunk_masks(m)
  upper = lax.broadcasted_iota(jnp.int32, (m, m), 0) <= lax.broadcasted_iota(jnp.int32, (m, m), 1)
  upper = (upper & same).astype(_BF16)
  lower_block = jnp.concatenate([causal.astype(_BF16), same.astype(_BF16)], axis=0)
  lane = lax.broadcasted_iota(jnp.int32, (m, SMALL_COLS), 1)
  sub = lax.broadcasted_iota(jnp.int32, (SMALL_ROWS, m), 0)

  def split3(v, axis):
    hi = v.astype(_BF16)
    rest = v - hi.astype(_F32)
    mid = rest.astype(_BF16)
    lo = (rest - mid.astype(_F32)).astype(_BF16)
    return jnp.concatenate([hi, mid, lo], axis=axis)

  for r in range(0, n, m):
    hr = h_scr[r:r + m, :]
    sm = _dot(hr, wsm_ref[...])
    g = -jnp.exp(smc_ref[0:1, :]) * _softplus(sm + smc_ref[1:2, :])
    sums = _dot(lower_block, split3(g, 1))
    sums = sums[:, :SMALL_COLS] + sums[:, SMALL_COLS:2 * SMALL_COLS] + sums[:, 2 * SMALL_COLS:]
    bg_ref[r:r + m, :] = jnp.where(lane < HEADS, _sigmoid(sm),
                                   jnp.where(lane < 2 * HEADS, sums[:m], sums[m:]))
    smt = _dot_nt(wsmt_ref[...], hr)
    gt = -jnp.exp(smct_ref[:, 0:1]) * _softplus(smt + smct_ref[:, 1:2])
    sums_t = _dot(split3(gt, 0), upper)
    sums_t = sums_t[:SMALL_ROWS] + sums_t[SMALL_ROWS:2 * SMALL_ROWS] + sums_t[2 * SMALL_ROWS:]
    bgt_ref[:, r:r + m] = jnp.where(sub < HEADS, _sigmoid(smt), sums_t)


def _mix_in(x2, npre, w_all, wglu, wsm, wsmt, smc, smct, pw1b, dww, layer, batch, seq):
  t = x2.shape[0]
  nt = seq // IN_ROWS
  row = lambda width: pl.BlockSpec((IN_ROWS, width), lambda b, i: (b * nt + i, 0))
  tiles = lambda count: pl.BlockSpec((count, IN_ROWS, LANES), lambda b, i: (0, b * nt + i, 0))
  weights = [wglu, wsm, wsmt, smct, dww, npre, smc, pw1b]
  wcat = _layer(layer, D_MODEL, 4 * HEADS * HEAD_DIM)
  out_shape = (
      jax.ShapeDtypeStruct((4 * HEADS, t, LANES), _F32),
      jax.ShapeDtypeStruct((CNV_TILES, t, LANES), _F32),
      jax.ShapeDtypeStruct((t, SMALL_COLS), _F32),
      jax.ShapeDtypeStruct((SMALL_ROWS, t), _F32),
  )
  out_specs = (
      tiles(4 * HEADS), tiles(CNV_TILES), row(SMALL_COLS),
      pl.BlockSpec((SMALL_ROWS, IN_ROWS), lambda b, i: (0, b * nt + i)),
  )
  return pl.pallas_call(
      _mix_in_body,
      grid=(batch, nt),
      in_specs=[row(D_MODEL), wcat] + [_resident(w.shape) for w in weights],
      out_specs=out_specs,
      out_shape=out_shape,
      scratch_shapes=[
          pltpu.VMEM((CNV_TILES, GLU_HALO + IN_ROWS, LANES), _F32),
          pltpu.VMEM((IN_ROWS, D_MODEL), _BF16),
      ],
      compiler_params=pltpu.CompilerParams(
          dimension_semantics=("arbitrary", "arbitrary"), vmem_limit_bytes=VMEM_LIMIT_BYTES),
      name="mix_in",
  )(x2, w_all, *weights)


def _gdn_body(q_ref, k_ref, v_ref, z_ref, bg_ref, bgt_ref, nw_ref, convw_ref, o_ref, state, ext):
  n = MIX_ROWS
  t = pl.program_id(1)
  heads = range(HEADS)

  @pl.when(t == 0)
  def _():
    state[...] = jnp.zeros((HEADS, HEAD_DIM, HEAD_DIM), _F32)
    ext[:, 0:QKV_HALO, :] = jnp.zeros((3 * HEADS, QKV_HALO, LANES), _F32)

  def conv(part, ref, j):
    c = part * HEADS + j
    ext[c, QKV_HALO:QKV_HALO + n, :] = ref[j]
    acc = jnp.zeros((n // SUBLANES, SUBLANES, LANES), _F32)
    for tap in range(GDN_CONV):
      start = QKV_HALO - (GDN_CONV - 1) + tap
      acc = acc + convw_ref[c, tap][None] * _vregs(ext[c, start:start + n, :])
    ext[c, 0:QKV_HALO, :] = ext[c, n:n + QKV_HALO, :]
    y = _silu(acc.reshape(n, LANES))
    if part < 2:
      y = y * lax.rsqrt(jnp.sum(y * y, axis=-1, keepdims=True) + 1e-6)
    if part == 0:
      y = y * (HEAD_DIM ** -0.5)
    return y

  bg = bg_ref[...]
  causal, _ = _chunk_masks(n)
  row = lax.broadcasted_iota(jnp.int32, (n, n), 0)
  col = lax.broadcasted_iota(jnp.int32, (n, n), 1)
  strict = causal & (col < row)
  eye = (row == col).astype(_F32)

  beta = [bg[:, j:j + 1] for j in heads]
  g_col = [bg[:, HEADS + j:HEADS + j + 1] for j in heads]
  g_last = [bg[:, 2 * HEADS + j:2 * HEADS + j + 1] for j in heads]
  qs, ks, vs, qkb, xs, ps = {}, {}, {}, {}, {}, {}
  us, ws, q_dec, k_dec = {}, {}, {}, {}

  def tile_stage(group):
    for j in group:
      q = conv(0, q_ref, j)
      k = conv(1, k_ref, j)
      v = conv(2, v_ref, j)
      g_row = bgt_ref[HEADS + j:HEADS + j + 1, :]
      d = jnp.where(causal, jnp.exp(jnp.where(causal, g_col[j] - g_row, 0.0)), 0.0)
      kb = k.astype(_BF16)
      qk = (_dot_nt(q.astype(_BF16), kb) * d).astype(_BF16)
      qkb[j] = [qk[c * CHUNK:(c + 1) * CHUNK, c * CHUNK:(c + 1) * CHUNK]
                for c in range(n // CHUNK)]
      a_mat = jnp.where(strict, _dot_nt(kb, kb) * d * beta[j], 0.0)
      xs[j] = eye - a_mat
      ps[j] = a_mat
      qs[j], ks[j], vs[j] = q, k, v

    for _ in range(int(math.log2(CHUNK)) - 1):
      for j in group:
        pb = ps[j].astype(_BF16)
        ps[j] = _dot(pb, pb)
        xs[j] = xs[j] + _dot(xs[j].astype(_BF16), ps[j].astype(_BF16))

    for j in group:
      e_g = jnp.exp(g_col[j])
      rhs = jnp.concatenate([vs[j] * beta[j], ks[j] * (beta[j] * e_g)], axis=-1).astype(_BF16)
      uw = _dot(xs[j].astype(_BF16), rhs)
      us[j] = uw[:, :HEAD_DIM]
      ws[j] = uw[:, HEAD_DIM:]
      q_dec[j] = qs[j] * e_g
      k_dec[j] = (ks[j] * jnp.exp(g_last[j] - g_col[j])).astype(_BF16)

  for first in range(0, HEADS, GDN_BATCH):
    tile_stage(range(first, first + GDN_BATCH))

  s = [state[j] for j in heads]
  for c in range(n // CHUNK):
    r = slice(c * CHUNK, (c + 1) * CHUNK)
    ws_qs = []
    for j in heads:
      wq = jnp.concatenate([ws[j][r], q_dec[j][r]], axis=0).astype(_BF16)
      ws_qs.append(_dot(wq, s[j].astype(_BF16)))
    for j in heads:
      v_new = (us[j][r] - ws_qs[j][:CHUNK]).astype(_BF16)
      o_c = ws_qs[j][CHUNK:] + _dot(qkb[j][c], v_new)
      d_c = jnp.exp(g_last[j][c * CHUNK:c * CHUNK + 1, :])
      s[j] = s[j] * d_c + _dot_tn(k_dec[j][r], v_new)
      o_n = _rms(o_c, nw_ref[...]) * _silu(z_ref[j, r, :])
      o_ref[r, j * HEAD_DIM:(j + 1) * HEAD_DIM] = o_n.astype(o_ref.dtype)
  for j in heads:
    state[j] = s[j]


def _gdn(proj, bg, bgt, norm_w, convw, batch, seq):
  t = bg.shape[0]
  nt = seq // MIX_ROWS

  def head_major(part):
    return pl.BlockSpec((HEADS, MIX_ROWS, LANES), lambda b, i: (part, b * nt + i, 0))

  return pl.pallas_call(
      _gdn_body,
      grid=(batch, nt),
      in_specs=[head_major(0), head_major(1), head_major(2), head_major(3),
                pl.BlockSpec((MIX_ROWS, SMALL_COLS), lambda b, i: (b * nt + i, 0)),
                pl.BlockSpec((SMALL_ROWS, MIX_ROWS), lambda b, i: (0, b * nt + i)),
                _resident((1, HEAD_DIM)), _resident(convw.shape)],
      out_specs=pl.BlockSpec((MIX_ROWS, D_MODEL), lambda b, i: (b * nt + i, 0)),
      out_shape=jax.ShapeDtypeStruct((t, D_MODEL), _BF16),
      scratch_shapes=[pltpu.VMEM((HEADS, HEAD_DIM, HEAD_DIM), _F32),
                      pltpu.VMEM((3 * HEADS, QKV_HALO + MIX_ROWS, LANES), _F32)],
      compiler_params=pltpu.CompilerParams(
          dimension_semantics=("arbitrary", "arbitrary"), vmem_limit_bytes=VMEM_LIMIT_BYTES),
      name="gdn",
  )(proj, proj, proj, proj, bg, bgt, norm_w, convw)


def _mix_out_body(x_ref, og_ref, cv_ref, npre_ref, wgat_ref, wgo_ref, wmo_ref, cwo_ref, dwb_ref,
                  lng_ref, lnb_ref, cbo_ref, npost_ref, o_ref):
  x = x_ref[...]
  gates = _sigmoid(_dot(_rms(x, npre_ref[...]).astype(_BF16), wgat_ref[...]))
  cv = jnp.concatenate([cv_ref[c] for c in range(CNV_TILES)], axis=-1) + dwb_ref[...]
  mu = jnp.mean(cv, axis=-1, keepdims=True)
  cen = cv - mu
  var = jnp.mean(cen * cen, axis=-1, keepdims=True)
  ln = cen * lax.rsqrt(var + LN_EPS) * lng_ref[...] + lnb_ref[...]
  y_b = _dot(_silu(ln).astype(_BF16), cwo_ref[...]) + cbo_ref[...]
  y_a = _dot(og_ref[...], wgo_ref[...])
  y = gates[:, :D_MODEL] * y_a + gates[:, D_MODEL:] * y_b
  y2 = _dot(y.astype(_BF16), wmo_ref[...])
  o_ref[...] = x + _rms(y2, npost_ref[...])


def _mix_out(x2, og, cv, npre, wgat, wgo, wmo, cwo, dwb, lng, lnb, cbo, npost, layer):
  t = x2.shape[0]
  row = pl.BlockSpec((OUT_ROWS, D_MODEL), lambda i: (i, 0))
  square = _layer(layer, D_MODEL, D_MODEL)
  vec = _resident((1, D_MODEL))
  return pl.pallas_call(
      _mix_out_body,
      grid=(t // OUT_ROWS,),
      in_specs=[row, row, pl.BlockSpec((CNV_TILES, OUT_ROWS, LANES), lambda i: (0, i, 0)),
                vec, _resident((D_MODEL, 2 * D_MODEL)), square, square, square,
                vec, vec, vec, vec, vec],
      out_specs=row,
      out_shape=jax.ShapeDtypeStruct((t, D_MODEL), _F32),
      compiler_params=pltpu.CompilerParams(
          dimension_semantics=("arbitrary",), vmem_limit_bytes=VMEM_LIMIT_BYTES),
      name="mix_out",
  )(x2, og, cv, npre, wgat, wgo, wmo, cwo, dwb, lng, lnb, cbo, npost)


def _row(v):
  return v.reshape(1, -1).astype(_F32)


def _lane_tiles(w):
  k, c = w.shape
  tiles = w.reshape(k, c // LANES, LANES).transpose(1, 0, 2).astype(_F32)
  return jnp.broadcast_to(tiles[:, :, None, :], (c // LANES, k, SUBLANES, LANES))


def kernel(x, ffn1_norm_pre, ffn1_norm_post, ffn1_w_in, ffn1_w_out, mix_norm_pre, mix_norm_post,
           mix_w_in, gdn_conv_w, gdn_a_log, gdn_dt_bias, gdn_norm_w, gdn_w_o, cnv_pw1_b, cnv_dw_w,
           cnv_dw_b, cnv_ln_g, cnv_ln_b, cnv_w_o, cnv_b_o, mix_w_out, ffn2_norm_pre,
           ffn2_norm_post, ffn2_w_in, ffn2_w_out):
  batch, seq, _ = x.shape
  depth = ffn1_w_in.shape[0]
  x2 = x.reshape(batch * seq, D_MODEL)
  o_qkv = 3 * HEADS * HEAD_DIM
  o_z = o_qkv + HEADS * HEAD_DIM
  o_b = o_z + HEADS
  o_a = o_b + HEADS
  o_glu = o_a + 2 * CNV_CH
  ffn1_in, ffn1_out = ffn1_w_in.astype(_BF16), ffn1_w_out.astype(_BF16)
  ffn2_in, ffn2_out = ffn2_w_in.astype(_BF16), ffn2_w_out.astype(_BF16)
  mix_in_w = mix_w_in.astype(_BF16)
  gdn_out, cnv_out, mix_out_w = (gdn_w_o.astype(_BF16), cnv_w_o.astype(_BF16),
                                 mix_w_out.astype(_BF16))
  for i in range(depth):
    x2 = _ffn(x2, _row(ffn1_norm_pre[i]), _row(ffn1_norm_post[i]), ffn1_in, ffn1_out, i)

    w = mix_in_w[i]
    w_beta = w[:, o_z:o_b]
    w_a = w[:, o_b:o_a]
    wsm = jnp.concatenate(
        [w_beta, w_a, w_a, jnp.zeros((D_MODEL, SMALL_COLS - 3 * HEADS), _BF16)], axis=1)
    wsmt = jnp.concatenate(
        [w_beta, w_a, w_a, jnp.zeros((D_MODEL, SMALL_ROWS - 3 * HEADS), _BF16)], axis=1).T
    pad = jnp.zeros((HEADS,), _F32)
    a_log = jnp.concatenate([pad, gdn_a_log[i], gdn_a_log[i]])
    dt_b = jnp.concatenate([pad, gdn_dt_bias[i], gdn_dt_bias[i]])
    smc = jnp.zeros((SUBLANES, SMALL_COLS), _F32)
    smc = smc.at[0, :3 * HEADS].set(a_log).at[1, :3 * HEADS].set(dt_b)
    smct = jnp.zeros((SMALL_ROWS, LANES), _F32)
    smct = smct.at[:3 * HEADS, 0].set(a_log).at[:3 * HEADS, 1].set(dt_b)

    npre = _row(mix_norm_pre[i])
    proj, cv, bg, bgt = _mix_in(
        x2, npre, mix_in_w, w[:, o_a:o_glu], wsm, wsmt, smc, smct, _row(cnv_pw1_b[i]),
        _lane_tiles(cnv_dw_w[i]), i, batch, seq)
    og = _gdn(proj, bg, bgt, _row(gdn_norm_w[i]), _lane_tiles(gdn_conv_w[i]), batch, seq)
    x2 = _mix_out(x2, og, cv, npre, w[:, o_glu:], gdn_out, mix_out_w, cnv_out,
                  _row(cnv_dw_b[i]), _row(cnv_ln_g[i]), _row(cnv_ln_b[i]), _row(cnv_b_o[i]),
                  _row(mix_norm_post[i]), i)

    x2 = _ffn(x2, _row(ffn2_norm_pre[i]), _row(ffn2_norm_post[i]), ffn2_in, ffn2_out, i)
  return x2.reshape(batch, seq, D_MODEL)
```

```python
import math

import jax
import jax.numpy as jnp
from jax import lax
from jax.experimental import pallas as pl
from jax.experimental.pallas import tpu as pltpu

D_MODEL = 1024
D_FF = 2816
HEADS = 8
HEAD_DIM = 128
GDN_CONV = 4
CNV_CH = 1024
CNV_K = 31
RMS_EPS = 1e-6
LN_EPS = 1e-5

LANES = 128
SUBLANES = 8
VMEM_LIMIT_BYTES = 56 * 1024 * 1024

FFN_ROWS = 512
FF_CHUNK = 2816
IN_ROWS = 512
MIX_ROWS = 256
PROJ_PIECE = 512
CONV_ROWS = 128
OUT_ROWS = 512
CHUNK = 64
GDN_BATCH = 4
QKV_HALO = SUBLANES
GLU_HALO = 4 * SUBLANES
SMALL_COLS = LANES
SMALL_ROWS = 4 * SUBLANES
CNV_TILES = CNV_CH // LANES

_BF16 = jnp.bfloat16
_F32 = jnp.float32


def _dot(a, b):
  return jnp.dot(a, b, preferred_element_type=_F32)


def _dot_nt(a, b):
  return lax.dot_general(a, b, (((1,), (1,)), ((), ())), preferred_element_type=_F32)


def _dot_tn(a, b):
  return lax.dot_general(a, b, (((0,), (0,)), ((), ())), preferred_element_type=_F32)


def _rms(x, w):
  ms = jnp.mean(x * x, axis=-1, keepdims=True)
  return x * lax.rsqrt(ms + RMS_EPS) * w


def _sigmoid(x):
  return 0.5 * jnp.tanh(0.5 * x) + 0.5


def _silu(x):
  half = 0.5 * x
  return half * jnp.tanh(half) + half


def _softplus(x):
  return jnp.maximum(x, 0.0) + jnp.log1p(jnp.exp(-jnp.abs(x)))


def _resident(shape):
  zeros = (0,) * len(shape)
  return pl.BlockSpec(shape, lambda *_: zeros, pipeline_mode=pl.Buffered(1))


def _layer(layer, rows, cols):
  return pl.BlockSpec((None, rows, cols), lambda *_: (layer, 0, 0), pipeline_mode=pl.Buffered(1))


def _vregs(x):
  return x.reshape(x.shape[0] // SUBLANES, SUBLANES, LANES)


def _ffn_body(x_ref, npre_ref, npost_ref, win_ref, wout_ref, o_ref):
  x = x_ref[...]
  h = _rms(x, npre_ref[...]).astype(_BF16)
  acc = jnp.zeros((FFN_ROWS, D_MODEL), _F32)
  for c in range(D_FF // FF_CHUNK):
    lo = c * FF_CHUNK
    gate = _dot(h, win_ref[:, lo:lo + FF_CHUNK])
    up = _dot(h, win_ref[:, D_FF + lo:D_FF + lo + FF_CHUNK])
    act = (_silu(gate) * up).astype(_BF16)
    acc = acc + _dot(act, wout_ref[lo:lo + FF_CHUNK, :])
  o_ref[...] = x + 0.5 * _rms(acc, npost_ref[...])


def _ffn(x2, npre, npost, w_in, w_out, layer):
  t = x2.shape[0]
  row = pl.BlockSpec((FFN_ROWS, D_MODEL), lambda i: (i, 0))
  return pl.pallas_call(
      _ffn_body,
      grid=(t // FFN_ROWS,),
      in_specs=[row, _resident((1, D_MODEL)), _resident((1, D_MODEL)),
                _layer(layer, D_MODEL, 2 * D_FF), _layer(layer, D_FF, D_MODEL)],
      out_specs=row,
      out_shape=jax.ShapeDtypeStruct((t, D_MODEL), _F32),
      compiler_params=pltpu.CompilerParams(
          dimension_semantics=("arbitrary",), vmem_limit_bytes=VMEM_LIMIT_BYTES),
      name="ffn",
  )(x2, npre, npost, w_in, w_out)


def _chunk_masks(n):
  shift = int(math.log2(CHUNK))
  row = lax.broadcasted_iota(jnp.int32, (n, n), 0)
  col = lax.broadcasted_iota(jnp.int32, (n, n), 1)
  same = lax.shift_right_logical(row, shift) == lax.shift_right_logical(col, shift)
  return same & (col <= row), same


def _mix_in_body(x_ref, wcat_ref, wglu_ref, wsm_ref, wsmt_ref, smct_ref, dww_ref,
                 npre_ref, smc_ref, pw1b_ref,
                 proj_ref, cv_ref, bg_ref, bgt_ref,
                 glu_ext, h_scr):
  n = IN_ROWS
  t = pl.program_id(1)

  @pl.when(t == 0)
  def _():
    glu_ext[:, 0:GLU_HALO, :] = jnp.zeros((CNV_TILES, GLU_HALO, LANES), _F32)

  h_scr[...] = _rms(x_ref[...], npre_ref[...]).astype(_BF16)
  h = h_scr[...]

  glu = _dot(h, wglu_ref[...]) + pw1b_ref[...]
  hh = glu[:, :CNV_CH] * _sigmoid(glu[:, CNV_CH:])
  for c in range(CNV_TILES):
    glu_ext[c, GLU_HALO:GLU_HALO + n, :] = hh[:, c * LANES:(c + 1) * LANES]

  tiles_per_piece = PROJ_PIECE // LANES
  reach = SUBLANES * ((CNV_K - 1) // SUBLANES)

  def conv_tile(c, carry):
    for r in range(0, n, CONV_ROWS):
      acc = jnp.zeros((CONV_ROWS // SUBLANES, SUBLANES, LANES), _F32)
      for s in range(SUBLANES):
        start = GLU_HALO - (CNV_K - 1) + s + r
        window = glu_ext[c, start:start + CONV_ROWS + reach, :]
        for j in range(s, CNV_K, SUBLANES):
          part = _vregs(window[j - s:j - s + CONV_ROWS, :])
          acc = acc + dww_ref[c, j][None] * part
      cv_ref[c, r:r + CONV_ROWS, :] = acc.reshape(CONV_ROWS, LANES)
    glu_ext[c, 0:GLU_HALO, :] = glu_ext[c, n:n + GLU_HALO, :]
    return carry

  for c in range(wcat_ref.shape[1] // PROJ_PIECE):
    p = _dot(h, wcat_ref[:, c * PROJ_PIECE:(c + 1) * PROJ_PIECE])
    for i in range(tiles_per_piece):
      proj_ref[c * tiles_per_piece + i] = p[:, i * LANES:(i + 1) * LANES]

  lax.fori_loop(0, CNV_TILES, conv_tile, 0)

  m = MIX_ROWS
  causal, same = _chunk_masks(m)
  upper = lax.broadcasted_iota(jnp.int32, (m, m), 0) <= lax.broadcasted_iota(jnp.int32, (m, m), 1)
  upper = (upper & same).astype(_BF16)
  lower_block = jnp.concatenate([causal.astype(_BF16), same.astype(_BF16)], axis=0)
  lane = lax.broadcasted_iota(jnp.int32, (m, SMALL_COLS), 1)
  sub = lax.broadcasted_iota(jnp.int32, (SMALL_ROWS, m), 0)

  def split3(v, axis):
    hi = v.astype(_BF16)
    rest = v - hi.astype(_F32)
    mid = rest.astype(_BF16)
    lo = (rest - mid.astype(_F32)).astype(_BF16)
    return jnp.concatenate([hi, mid, lo], axis=axis)

  for r in range(0, n, m):
    hr = h_scr[r:r + m, :]
    sm = _dot(hr, wsm_ref[...])
    g = -jnp.exp(smc_ref[0:1, :]) * _softplus(sm + smc_ref[1:2, :])
    sums = _dot(lower_block, split3(g, 1))
    sums = sums[:, :SMALL_COLS] + sums[:, SMALL_COLS:2 * SMALL_COLS] + sums[:, 2 * SMALL_COLS:]
    bg_ref[r:r + m, :] = jnp.where(lane < HEADS, _sigmoid(sm),
                                   jnp.where(lane < 2 * HEADS, sums[:m], sums[m:]))
    smt = _dot_nt(wsmt_ref[...], hr)
    gt = -jnp.exp(smct_ref[:, 0:1]) * _softplus(smt + smct_ref[:, 1:2])
    sums_t = _dot(split3(gt, 0), upper)
    sums_t = sums_t[:SMALL_ROWS] + sums_t[SMALL_ROWS:2 * SMALL_ROWS] + sums_t[2 * SMALL_ROWS:]
    bgt_ref[:, r:r + m] = jnp.where(sub < HEADS, _sigmoid(smt), sums_t)


def _mix_in(x2, npre, w_all, wglu, wsm, wsmt, smc, smct, pw1b, dww, layer, batch, seq):
  t = x2.shape[0]
  nt = seq // IN_ROWS
  row = lambda width: pl.BlockSpec((IN_ROWS, width), lambda b, i: (b * nt + i, 0))
  tiles = lambda count: pl.BlockSpec((count, IN_ROWS, LANES), lambda b, i: (0, b * nt + i, 0))
  weights = [wglu, wsm, wsmt, smct, dww, npre, smc, pw1b]
  wcat = _layer(layer, D_MODEL, 4 * HEADS * HEAD_DIM)
  out_shape = (
      jax.ShapeDtypeStruct((4 * HEADS, t, LANES), _F32),
      jax.ShapeDtypeStruct((CNV_TILES, t, LANES), _F32),
      jax.ShapeDtypeStruct((t, SMALL_COLS), _F32),
      jax.ShapeDtypeStruct((SMALL_ROWS, t), _F32),
  )
  out_specs = (
      tiles(4 * HEADS), tiles(CNV_TILES), row(SMALL_COLS),
      pl.BlockSpec((SMALL_ROWS, IN_ROWS), lambda b, i: (0, b * nt + i)),
  )
  return pl.pallas_call(
      _mix_in_body,
      grid=(batch, nt),
      in_specs=[row(D_MODEL), wcat] + [_resident(w.shape) for w in weights],
      out_specs=out_specs,
      out_shape=out_shape,
      scratch_shapes=[
          pltpu.VMEM((CNV_TILES, GLU_HALO + IN_ROWS, LANES), _F32),
          pltpu.VMEM((IN_ROWS, D_MODEL), _BF16),
      ],
      compiler_params=pltpu.CompilerParams(
          dimension_semantics=("arbitrary", "arbitrary"), vmem_limit_bytes=VMEM_LIMIT_BYTES),
      name="mix_in",
  )(x2, w_all, *weights)


def _gdn_body(q_ref, k_ref, v_ref, z_ref, bg_ref, bgt_ref, nw_ref, convw_ref, o_ref, state, ext):
  n = MIX_ROWS
  t = pl.program_id(1)
  heads = range(HEADS)

  @pl.when(t == 0)
  def _():
    state[...] = jnp.zeros((HEADS, HEAD_DIM, HEAD_DIM), _F32)
    ext[:, 0:QKV_HALO, :] = jnp.zeros((3 * HEADS, QKV_HALO, LANES), _F32)

  def conv(part, ref, j):
    c = part * HEADS + j
    ext[c, QKV_HALO:QKV_HALO + n, :] = ref[j]
    acc = jnp.zeros((n // SUBLANES, SUBLANES, LANES), _F32)
    for tap in range(GDN_CONV):
      start = QKV_HALO - (GDN_CONV - 1) + tap
      acc = acc + convw_ref[c, tap][None] * _vregs(ext[c, start:start + n, :])
    ext[c, 0:QKV_HALO, :] = ext[c, n:n + QKV_HALO, :]
    y = _silu(acc.reshape(n, LANES))
    if part < 2:
      y = y * lax.rsqrt(jnp.sum(y * y, axis=-1, keepdims=True) + 1e-6)
    if part == 0:
      y = y * (HEAD_DIM ** -0.5)
    return y

  bg = bg_ref[...]
  causal, _ = _chunk_masks(n)
  row = lax.broadcasted_iota(jnp.int32, (n, n), 0)
  col = lax.broadcasted_iota(jnp.int32, (n, n), 1)
  strict = causal & (col < row)
  eye = (row == col).astype(_F32)

  beta = [bg[:, j:j + 1] for j in heads]
  g_col = [bg[:, HEADS + j:HEADS + j + 1] for j in heads]
  g_last = [bg[:, 2 * HEADS + j:2 * HEADS + j + 1] for j in heads]
  qs, ks, vs, qkb, xs, ps = {}, {}, {}, {}, {}, {}
  us, ws, q_dec, k_dec = {}, {}, {}, {}

  def tile_stage(group):
    for j in group:
      q = conv(0, q_ref, j)
      k = conv(1, k_ref, j)
      v = conv(2, v_ref, j)
      g_row = bgt_ref[HEADS + j:HEADS + j + 1, :]
      d = jnp.where(causal, jnp.exp(jnp.where(causal, g_col[j] - g_row, 0.0)), 0.0)
      kb = k.astype(_BF16)
      qk = (_dot_nt(q.astype(_BF16), kb) * d).astype(_BF16)
      qkb[j] = [qk[c * CHUNK:(c + 1) * CHUNK, c * CHUNK:(c + 1) * CHUNK]
                for c in range(n // CHUNK)]
      a_mat = jnp.where(strict, _dot_nt(kb, kb) * d * beta[j], 0.0)
      xs[j] = eye - a_mat
      ps[j] = a_mat
      qs[j], ks[j], vs[j] = q, k, v

    for _ in range(int(math.log2(CHUNK)) - 1):
      for j in group:
        pb = ps[j].astype(_BF16)
        ps[j] = _dot(pb, pb)
        xs[j] = xs[j] + _dot(xs[j].astype(_BF16), ps[j].astype(_BF16))

    for j in group:
      e_g = jnp.exp(g_col[j])
      rhs = jnp.concatenate([vs[j] * beta[j], ks[j] * (beta[j] * e_g)], axis=-1).astype(_BF16)
      uw = _dot(xs[j].astype(_BF16), rhs)
      us[j] = uw[:, :HEAD_DIM]
      ws[j] = uw[:, HEAD_DIM:]
      q_dec[j] = qs[j] * e_g
      k_dec[j] = (ks[j] * jnp.exp(g_last[j] - g_col[j])).astype(_BF16)

  for first in range(0, HEADS, GDN_BATCH):
    tile_stage(range(first, first + GDN_BATCH))

  s = [state[j] for j in heads]
  for c in range(n // CHUNK):
    r = slice(c * CHUNK, (c + 1) * CHUNK)
    ws_qs = []
    for j in heads:
      wq = jnp.concatenate([ws[j][r], q_dec[j][r]], axis=0).astype(_BF16)
      ws_qs.append(_dot(wq, s[j].astype(_BF16)))
    for j in heads:
      v_new = (us[j][r] - ws_qs[j][:CHUNK]).astype(_BF16)
      o_c = ws_qs[j][CHUNK:] + _dot(qkb[j][c], v_new)
      d_c = jnp.exp(g_last[j][c * CHUNK:c * CHUNK + 1, :])
      s[j] = s[j] * d_c + _dot_tn(k_dec[j][r], v_new)
      o_n = _rms(o_c, nw_ref[...]) * _silu(z_ref[j, r, :])
      o_ref[r, j * HEAD_DIM:(j + 1) * HEAD_DIM] = o_n.astype(o_ref.dtype)
  for j in heads:
    state[j] = s[j]


def _gdn(proj, bg, bgt, norm_w, convw, batch, seq):
  t = bg.shape[0]
  nt = seq // MIX_ROWS

  def head_major(part):
    return pl.BlockSpec((HEADS, MIX_ROWS, LANES), lambda b, i: (part, b * nt + i, 0))

  return pl.pallas_call(
      _gdn_body,
      grid=(batch, nt),
      in_specs=[head_major(0), head_major(1), head_major(2), head_major(3),
                pl.BlockSpec((MIX_ROWS, SMALL_COLS), lambda b, i: (b * nt + i, 0)),
                pl.BlockSpec((SMALL_ROWS, MIX_ROWS), lambda b, i: (0, b * nt + i)),
                _resident((1, HEAD_DIM)), _resident(convw.shape)],
      out_specs=pl.BlockSpec((MIX_ROWS, D_MODEL), lambda b, i: (b * nt + i, 0)),
      out_shape=jax.ShapeDtypeStruct((t, D_MODEL), _BF16),
      scratch_shapes=[pltpu.VMEM((HEADS, HEAD_DIM, HEAD_DIM), _F32),
                      pltpu.VMEM((3 * HEADS, QKV_HALO + MIX_ROWS, LANES), _F32)],
      compiler_params=pltpu.CompilerParams(
          dimension_semantics=("arbitrary", "arbitrary"), vmem_limit_bytes=VMEM_LIMIT_BYTES),
      name="gdn",
  )(proj, proj, proj, proj, bg, bgt, norm_w, convw)


def _mix_out_body(x_ref, og_ref, cv_ref, npre_ref, wgat_ref, wgo_ref, wmo_ref, cwo_ref, dwb_ref,
                  lng_ref, lnb_ref, cbo_ref, npost_ref, o_ref):
  x = x_ref[...]
  gates = _sigmoid(_dot(_rms(x, npre_ref[...]).astype(_BF16), wgat_ref[...]))
  cv = jnp.concatenate([cv_ref[c] for c in range(CNV_TILES)], axis=-1) + dwb_ref[...]
  mu = jnp.mean(cv, axis=-1, keepdims=True)
  cen = cv - mu
  var = jnp.mean(cen * cen, axis=-1, keepdims=True)
  ln = cen * lax.rsqrt(var + LN_EPS) * lng_ref[...] + lnb_ref[...]
  y_b = _dot(_silu(ln).astype(_BF16), cwo_ref[...]) + cbo_ref[...]
  y_a = _dot(og_ref[...], wgo_ref[...])
  y = gates[:, :D_MODEL] * y_a + gates[:, D_MODEL:] * y_b
  y2 = _dot(y.astype(_BF16), wmo_ref[...])
  o_ref[...] = x + _rms(y2, npost_ref[...])


def _mix_out(x2, og, cv, npre, wgat, wgo, wmo, cwo, dwb, lng, lnb, cbo, npost, layer):
  t = x2.shape[0]
  row = pl.BlockSpec((OUT_ROWS, D_MODEL), lambda i: (i, 0))
  square = _layer(layer, D_MODEL, D_MODEL)
  vec = _resident((1, D_MODEL))
  return pl.pallas_call(
      _mix_out_body,
      grid=(t // OUT_ROWS,),
      in_specs=[row, row, pl.BlockSpec((CNV_TILES, OUT_ROWS, LANES), lambda i: (0, i, 0)),
                vec, _resident((D_MODEL, 2 * D_MODEL)), square, square, square,
                vec, vec, vec, vec, vec],
      out_specs=row,
      out_shape=jax.ShapeDtypeStruct((t, D_MODEL), _F32),
      compiler_params=pltpu.CompilerParams(
          dimension_semantics=("arbitrary",), vmem_limit_bytes=VMEM_LIMIT_BYTES),
      name="mix_out",
  )(x2, og, cv, npre, wgat, wgo, wmo, cwo, dwb, lng, lnb, cbo, npost)


def _row(v):
  return v.reshape(1, -1).astype(_F32)


def _lane_tiles(w):
  k, c = w.shape
  tiles = w.reshape(k, c // LANES, LANES).transpose(1, 0, 2).astype(_F32)
  return jnp.broadcast_to(tiles[:, :, None, :], (c // LANES, k, SUBLANES, LANES))


def kernel(x, ffn1_norm_pre, ffn1_norm_post, ffn1_w_in, ffn1_w_out, mix_norm_pre, mix_norm_post,
           mix_w_in, gdn_conv_w, gdn_a_log, gdn_dt_bias, gdn_norm_w, gdn_w_o, cnv_pw1_b, cnv_dw_w,
           cnv_dw_b, cnv_ln_g, cnv_ln_b, cnv_w_o, cnv_b_o, mix_w_out, ffn2_norm_pre,
           ffn2_norm_post, ffn2_w_in, ffn2_w_out):
  batch, seq, _ = x.shape
  depth = ffn1_w_in.shape[0]
  x2 = x.reshape(batch * seq, D_MODEL)
  o_qkv = 3 * HEADS * HEAD_DIM
  o_z = o_qkv + HEADS * HEAD_DIM
  o_b = o_z + HEADS
  o_a = o_b + HEADS
  o_glu = o_a + 2 * CNV_CH
  ffn1_in, ffn1_out = ffn1_w_in.astype(_BF16), ffn1_w_out.astype(_BF16)
  ffn2_in, ffn2_out = ffn2_w_in.astype(_BF16), ffn2_w_out.astype(_BF16)
  mix_in_w = mix_w_in.astype(_BF16)
  gdn_out, cnv_out, mix_out_w = (gdn_w_o.astype(_BF16), cnv_w_o.astype(_BF16),
                                 mix_w_out.astype(_BF16))
  for i in range(depth):
    x2 = _ffn(x2, _row(ffn1_norm_pre[i]), _row(ffn1_norm_post[i]), ffn1_in, ffn1_out, i)

    w_beta = mix_in_w[i, :, o_z:o_b]
    w_a = mix_in_w[i, :, o_b:o_a]
    wsm = jnp.concatenate(
        [w_beta, w_a, w_a, jnp.zeros((D_MODEL, SMALL_COLS - 3 * HEADS), _BF16)], axis=1)
    wsmt = jnp.concatenate(
        [w_beta, w_a, w_a, jnp.zeros((D_MODEL, SMALL_ROWS - 3 * HEADS), _BF16)], axis=1).T
    pad = jnp.zeros((HEADS,), _F32)
    a_log = jnp.concatenate([pad, gdn_a_log[i], gdn_a_log[i]])
    dt_b = jnp.concatenate([pad, gdn_dt_bias[i], gdn_dt_bias[i]])
    smc = jnp.zeros((SUBLANES, SMALL_COLS), _F32)
    smc = smc.at[0, :3 * HEADS].set(a_log).at[1, :3 * HEADS].set(dt_b)
    smct = jnp.zeros((SMALL_ROWS, LANES), _F32)
    smct = smct.at[:3 * HEADS, 0].set(a_log).at[:3 * HEADS, 1].set(dt_b)

    npre = _row(mix_norm_pre[i])
    proj, cv, bg, bgt = _mix_in(
        x2, npre, mix_in_w, mix_in_w[i, :, o_a:o_glu], wsm, wsmt, smc, smct,
        _row(cnv_pw1_b[i]), _lane_tiles(cnv_dw_w[i]), i, batch, seq)
    og = _gdn(proj, bg, bgt, _row(gdn_norm_w[i]), _lane_tiles(gdn_conv_w[i]), batch, seq)
    x2 = _mix_out(x2, og, cv, npre, mix_in_w[i, :, o_glu:], gdn_out, mix_out_w, cnv_out,
                  _row(cnv_dw_b[i]), _row(cnv_ln_g[i]), _row(cnv_ln_b[i]), _row(cnv_b_o[i]),
                  _row(mix_norm_post[i]), i)

    x2 = _ffn(x2, _row(ffn2_norm_pre[i]), _row(ffn2_norm_post[i]), ffn2_in, ffn2_out, i)
  return x2.reshape(batch, seq, D_MODEL)
```

```python
import math

import jax
import jax.numpy as jnp
from jax import lax
from jax.experimental import pallas as pl
from jax.experimental.pallas import tpu as pltpu

D_MODEL = 1024
D_FF = 2816
HEADS = 8
HEAD_DIM = 128
GDN_CONV = 4
CNV_CH = 1024
CNV_K = 31
RMS_EPS = 1e-6
LN_EPS = 1e-5

LANES = 128
SUBLANES = 8
VMEM_LIMIT_BYTES = 56 * 1024 * 1024

FFN_ROWS = 512
IN_ROWS = 512
MIX_ROWS = 256
PROJ_PIECE = 512
CONV_ROWS = 128
OUT_ROWS = 512
CHUNK = 64
GDN_BATCH = 4
QKV_HALO = SUBLANES
GLU_HALO = 4 * SUBLANES
SMALL_COLS = LANES
SMALL_ROWS = 4 * SUBLANES
CNV_TILES = CNV_CH // LANES

_BF16 = jnp.bfloat16
_F32 = jnp.float32


def _dot(a, b):
  return jnp.dot(a, b, preferred_element_type=_F32)


def _dot_nt(a, b):
  return lax.dot_general(a, b, (((1,), (1,)), ((), ())), preferred_element_type=_F32)


def _dot_tn(a, b):
  return lax.dot_general(a, b, (((0,), (0,)), ((), ())), preferred_element_type=_F32)


def _rms(x, w):
  ms = jnp.mean(x * x, axis=-1, keepdims=True)
  return x * lax.rsqrt(ms + RMS_EPS) * w


def _sigmoid(x):
  return 0.5 * jnp.tanh(0.5 * x) + 0.5


def _silu(x):
  half = 0.5 * x
  return half * jnp.tanh(half) + half


def _softplus(x):
  return jnp.maximum(x, 0.0) + jnp.log1p(jnp.exp(-jnp.abs(x)))


def _resident(shape):
  zeros = (0,) * len(shape)
  return pl.BlockSpec(shape, lambda *_: zeros, pipeline_mode=pl.Buffered(1))


def _layer(layer, rows, cols):
  return pl.BlockSpec((None, rows, cols), lambda *_: (layer, 0, 0), pipeline_mode=pl.Buffered(1))


def _vregs(x):
  return x.reshape(x.shape[0] // SUBLANES, SUBLANES, LANES)


def _ffn_body(x_ref, npre_ref, npost_ref, win_ref, wout_ref, o_ref):
  x = x_ref[...]
  h = _rms(x, npre_ref[...]).astype(_BF16)
  gate = _dot(h, win_ref[:, :D_FF])
  up = _dot(h, win_ref[:, D_FF:])
  f = _dot((_silu(gate) * up).astype(_BF16), wout_ref[...])
  o_ref[...] = x + 0.5 * _rms(f, npost_ref[...])


def _ffn(x2, npre, npost, w_in, w_out, layer):
  t = x2.shape[0]
  row = pl.BlockSpec((FFN_ROWS, D_MODEL), lambda i: (i, 0))
  return pl.pallas_call(
      _ffn_body,
      grid=(t // FFN_ROWS,),
      in_specs=[row, _resident((1, D_MODEL)), _resident((1, D_MODEL)),
                _layer(layer, D_MODEL, 2 * D_FF), _layer(layer, D_FF, D_MODEL)],
      out_specs=row,
      out_shape=jax.ShapeDtypeStruct((t, D_MODEL), _F32),
      compiler_params=pltpu.CompilerParams(
          dimension_semantics=("arbitrary",), vmem_limit_bytes=VMEM_LIMIT_BYTES),
      name="ffn",
  )(x2, npre, npost, w_in, w_out)


def _chunk_masks(n):
  shift = int(math.log2(CHUNK))
  row = lax.broadcasted_iota(jnp.int32, (n, n), 0)
  col = lax.broadcasted_iota(jnp.int32, (n, n), 1)
  same = lax.shift_right_logical(row, shift) == lax.shift_right_logical(col, shift)
  return same & (col <= row), same


def _mix_in_body(x_ref, wcat_ref, wglu_ref, wsm_ref, wsmt_ref, smct_ref, dww_ref,
                 npre_ref, smc_ref, pw1b_ref,
                 proj_ref, cv_ref, bg_ref, bgt_ref,
                 glu_ext, h_scr):
  n = IN_ROWS
  t = pl.program_id(1)

  @pl.when(t == 0)
  def _():
    glu_ext[:, 0:GLU_HALO, :] = jnp.zeros((CNV_TILES, GLU_HALO, LANES), _F32)

  h_scr[...] = _rms(x_ref[...], npre_ref[...]).astype(_BF16)
  h = h_scr[...]

  glu = _dot(h, wglu_ref[...]) + pw1b_ref[...]
  hh = glu[:, :CNV_CH] * _sigmoid(glu[:, CNV_CH:])
  for c in range(CNV_TILES):
    glu_ext[c, GLU_HALO:GLU_HALO + n, :] = hh[:, c * LANES:(c + 1) * LANES]

  tiles_per_piece = PROJ_PIECE // LANES
  reach = SUBLANES * ((CNV_K - 1) // SUBLANES)

  def conv_tile(c, carry):
    for r in range(0, n, CONV_ROWS):
      acc = jnp.zeros((CONV_ROWS // SUBLANES, SUBLANES, LANES), _F32)
      for s in range(SUBLANES):
        start = GLU_HALO - (CNV_K - 1) + s + r
        window = glu_ext[c, start:start + CONV_ROWS + reach, :]
        for j in range(s, CNV_K, SUBLANES):
          part = _vregs(window[j - s:j - s + CONV_ROWS, :])
          acc = acc + dww_ref[c, j][None] * part
      cv_ref[c, r:r + CONV_ROWS, :] = acc.reshape(CONV_ROWS, LANES)
    glu_ext[c, 0:GLU_HALO, :] = glu_ext[c, n:n + GLU_HALO, :]
    return carry

  for c in range(wcat_ref.shape[1] // PROJ_PIECE):
    p = _dot(h, wcat_ref[:, c * PROJ_PIECE:(c + 1) * PROJ_PIECE])
    for i in range(tiles_per_piece):
      proj_ref[c * tiles_per_piece + i] = p[:, i * LANES:(i + 1) * LANES]

  lax.fori_loop(0, CNV_TILES, conv_tile, 0)

  m = MIX_ROWS
  causal, same = _chunk_masks(m)
  upper = lax.broadcasted_iota(jnp.int32, (m, m), 0) <= lax.broadcasted_iota(jnp.int32, (m, m), 1)
  upper = (upper & same).astype(_BF16)
  lower_block = jnp.concatenate([causal.astype(_BF16), same.astype(_BF16)], axis=0)
  lane = lax.broadcasted_iota(jnp.int32, (m, SMALL_COLS), 1)
  sub = lax.broadcasted_iota(jnp.int32, (SMALL_ROWS, m), 0)

  def split3(v, axis):
    hi = v.astype(_BF16)
    rest = v - hi.astype(_F32)
    mid = rest.astype(_BF16)
    lo = (rest - mid.astype(_F32)).astype(_BF16)
    return jnp.concatenate([hi, mid, lo], axis=axis)

  for r in range(0, n, m):
    hr = h_scr[r:r + m, :]
    sm = _dot(hr, wsm_ref[...])
    g = -jnp.exp(smc_ref[0:1, :]) * _softplus(sm + smc_ref[1:2, :])
    sums = _dot(lower_block, split3(g, 1))
    sums = sums[:, :SMALL_COLS] + sums[:, SMALL_COLS:2 * SMALL_COLS] + sums[:, 2 * SMALL_COLS:]
    bg_ref[r:r + m, :] = jnp.where(lane < HEADS, _sigmoid(sm),
                                   jnp.where(lane < 2 * HEADS, sums[:m], sums[m:]))
    smt = _dot_nt(wsmt_ref[...], hr)
    gt = -jnp.exp(smct_ref[:, 0:1]) * _softplus(smt + smct_ref[:, 1:2])
    sums_t = _dot(split3(gt, 0), upper)
    sums_t = sums_t[:SMALL_ROWS] + sums_t[SMALL_ROWS:2 * SMALL_ROWS] + sums_t[2 * SMALL_ROWS:]
    bgt_ref[:, r:r + m] = jnp.where(sub < HEADS, _sigmoid(smt), sums_t)


def _mix_in(x2, npre, w_all, wglu, wsm, wsmt, smc, smct, pw1b, dww, layer, batch, seq):
  t = x2.shape[0]
  nt = seq // IN_ROWS
  row = lambda width: pl.BlockSpec((IN_ROWS, width), lambda b, i: (b * nt + i, 0))
  tiles = lambda count: pl.BlockSpec((count, IN_ROWS, LANES), lambda b, i: (0, b * nt + i, 0))
  weights = [wglu, wsm, wsmt, smct, dww, npre, smc, pw1b]
  wcat = _layer(layer, D_MODEL, 4 * HEADS * HEAD_DIM)
  out_shape = (
      jax.ShapeDtypeStruct((4 * HEADS, t, LANES), _F32),
      jax.ShapeDtypeStruct((CNV_TILES, t, LANES), _F32),
      jax.ShapeDtypeStruct((t, SMALL_COLS), _F32),
      jax.ShapeDtypeStruct((SMALL_ROWS, t), _F32),
  )
  out_specs = (
      tiles(4 * HEADS), tiles(CNV_TILES), row(SMALL_COLS),
      pl.BlockSpec((SMALL_ROWS, IN_ROWS), lambda b, i: (0, b * nt + i)),
  )
  return pl.pallas_call(
      _mix_in_body,
      grid=(batch, nt),
      in_specs=[row(D_MODEL), wcat] + [_resident(w.shape) for w in weights],
      out_specs=out_specs,
      out_shape=out_shape,
      scratch_shapes=[
          pltpu.VMEM((CNV_TILES, GLU_HALO + IN_ROWS, LANES), _F32),
          pltpu.VMEM((IN_ROWS, D_MODEL), _BF16),
      ],
      compiler_params=pltpu.CompilerParams(
          dimension_semantics=("arbitrary", "arbitrary"), vmem_limit_bytes=VMEM_LIMIT_BYTES),
      name="mix_in",
  )(x2, w_all, *weights)


def _gdn_body(q_ref, k_ref, v_ref, z_ref, bg_ref, bgt_ref, nw_ref, convw_ref, o_ref, state, ext):
  n = MIX_ROWS
  t = pl.program_id(1)
  heads = range(HEADS)

  @pl.when(t == 0)
  def _():
    state[...] = jnp.zeros((HEADS, HEAD_DIM, HEAD_DIM), _F32)
    ext[:, 0:QKV_HALO, :] = jnp.zeros((3 * HEADS, QKV_HALO, LANES), _F32)

  def conv(part, ref, j):
    c = part * HEADS + j
    ext[c, QKV_HALO:QKV_HALO + n, :] = ref[j]
    acc = jnp.zeros((n // SUBLANES, SUBLANES, LANES), _F32)
    for tap in range(GDN_CONV):
      start = QKV_HALO - (GDN_CONV - 1) + tap
      acc = acc + convw_ref[c, tap][None] * _vregs(ext[c, start:start + n, :])
    ext[c, 0:QKV_HALO, :] = ext[c, n:n + QKV_HALO, :]
    y = _silu(acc.reshape(n, LANES))
    if part < 2:
      y = y * lax.rsqrt(jnp.sum(y * y, axis=-1, keepdims=True) + 1e-6)
    if part == 0:
      y = y * (HEAD_DIM ** -0.5)
    return y

  bg = bg_ref[...]
  causal, _ = _chunk_masks(n)
  row = lax.broadcasted_iota(jnp.int32, (n, n), 0)
  col = lax.broadcasted_iota(jnp.int32, (n, n), 1)
  strict = causal & (col < row)
  eye = (row == col).astype(_F32)

  beta = [bg[:, j:j + 1] for j in heads]
  g_col = [bg[:, HEADS + j:HEADS + j + 1] for j in heads]
  g_last = [bg[:, 2 * HEADS + j:2 * HEADS + j + 1] for j in heads]
  qs, ks, vs, qkb, xs, ps = {}, {}, {}, {}, {}, {}
  us, ws, q_dec, k_dec = {}, {}, {}, {}

  def tile_stage(group):
    for j in group:
      q = conv(0, q_ref, j)
      k = conv(1, k_ref, j)
      v = conv(2, v_ref, j)
      g_row = bgt_ref[HEADS + j:HEADS + j + 1, :]
      d = jnp.where(causal, jnp.exp(jnp.where(causal, g_col[j] - g_row, 0.0)), 0.0)
      kb = k.astype(_BF16)
      qk = (_dot_nt(q.astype(_BF16), kb) * d).astype(_BF16)
      qkb[j] = [qk[c * CHUNK:(c + 1) * CHUNK, c * CHUNK:(c + 1) * CHUNK]
                for c in range(n // CHUNK)]
      a_mat = jnp.where(strict, _dot_nt(kb, kb) * d * beta[j], 0.0)
      xs[j] = eye - a_mat
      ps[j] = a_mat
      qs[j], ks[j], vs[j] = q, k, v

    for _ in range(int(math.log2(CHUNK)) - 1):
      for j in group:
        pb = ps[j].astype(_BF16)
        ps[j] = _dot(pb, pb)
        xs[j] = xs[j] + _dot(xs[j].astype(_BF16), ps[j].astype(_BF16))

    for j in group:
      e_g = jnp.exp(g_col[j])
      rhs = jnp.concatenate([vs[j] * beta[j], ks[j] * (beta[j] * e_g)], axis=-1).astype(_BF16)
      uw = _dot(xs[j].astype(_BF16), rhs)
      us[j] = uw[:, :HEAD_DIM]
      ws[j] = uw[:, HEAD_DIM:]
      q_dec[j] = qs[j] * e_g
      k_dec[j] = (ks[j] * jnp.exp(g_last[j] - g_col[j])).astype(_BF16)

  for first in range(0, HEADS, GDN_BATCH):
    tile_stage(range(first, first + GDN_BATCH))

  s = [state[j] for j in heads]
  for c in range(n // CHUNK):
    r = slice(c * CHUNK, (c + 1) * CHUNK)
    ws_qs = []
    for j in heads:
      wq = jnp.concatenate([ws[j][r], q_dec[j][r]], axis=0).astype(_BF16)
      ws_qs.append(_dot(wq, s[j].astype(_BF16)))
    for j in heads:
      v_new = (us[j][r] - ws_qs[j][:CHUNK]).astype(_BF16)
      o_c = ws_qs[j][CHUNK:] + _dot(qkb[j][c], v_new)
      d_c = jnp.exp(g_last[j][c * CHUNK:c * CHUNK + 1, :])
      s[j] = s[j] * d_c + _dot_tn(k_dec[j][r], v_new)
      o_n = _rms(o_c, nw_ref[...]) * _silu(z_ref[j, r, :])
      o_ref[r, j * HEAD_DIM:(j + 1) * HEAD_DIM] = o_n.astype(o_ref.dtype)
  for j in heads:
    state[j] = s[j]


def _gdn(proj, bg, bgt, norm_w, convw, batch, seq):
  t = bg.shape[0]
  nt = seq // MIX_ROWS

  def head_major(part):
    return pl.BlockSpec((HEADS, MIX_ROWS, LANES), lambda b, i: (part, b * nt + i, 0))

  return pl.pallas_call(
      _gdn_body,
      grid=(batch, nt),
      in_specs=[head_major(0), head_major(1), head_major(2), head_major(3),
                pl.BlockSpec((MIX_ROWS, SMALL_COLS), lambda b, i: (b * nt + i, 0)),
                pl.BlockSpec((SMALL_ROWS, MIX_ROWS), lambda b, i: (0, b * nt + i)),
                _resident((1, HEAD_DIM)), _resident(convw.shape)],
      out_specs=pl.BlockSpec((MIX_ROWS, D_MODEL), lambda b, i: (b * nt + i, 0)),
      out_shape=jax.ShapeDtypeStruct((t, D_MODEL), _BF16),
      scratch_shapes=[pltpu.VMEM((HEADS, HEAD_DIM, HEAD_DIM), _F32),
                      pltpu.VMEM((3 * HEADS, QKV_HALO + MIX_ROWS, LANES), _F32)],
      compiler_params=pltpu.CompilerParams(
          dimension_semantics=("arbitrary", "arbitrary"), vmem_limit_bytes=VMEM_LIMIT_BYTES),
      name="gdn",
  )(proj, proj, proj, proj, bg, bgt, norm_w, convw)


def _mix_out_body(x_ref, og_ref, cv_ref, npre_ref, wgat_ref, wgo_ref, wmo_ref, cwo_ref, dwb_ref,
                  lng_ref, lnb_ref, cbo_ref, npost_ref, o_ref):
  x = x_ref[...]
  gates = _sigmoid(_dot(_rms(x, npre_ref[...]).astype(_BF16), wgat_ref[...]))
  cv = jnp.concatenate([cv_ref[c] for c in range(CNV_TILES)], axis=-1) + dwb_ref[...]
  mu = jnp.mean(cv, axis=-1, keepdims=True)
  cen = cv - mu
  var = jnp.mean(cen * cen, axis=-1, keepdims=True)
  ln = cen * lax.rsqrt(var + LN_EPS) * lng_ref[...] + lnb_ref[...]
  y_b = _dot(_silu(ln).astype(_BF16), cwo_ref[...]) + cbo_ref[...]
  y_a = _dot(og_ref[...], wgo_ref[...])
  y = gates[:, :D_MODEL] * y_a + gates[:, D_MODEL:] * y_b
  y2 = _dot(y.astype(_BF16), wmo_ref[...])
  o_ref[...] = x + _rms(y2, npost_ref[...])


def _mix_out(x2, og, cv, npre, wgat, wgo, wmo, cwo, dwb, lng, lnb, cbo, npost, layer):
  t = x2.shape[0]
  row = pl.BlockSpec((OUT_ROWS, D_MODEL), lambda i: (i, 0))
  square = _layer(layer, D_MODEL, D_MODEL)
  vec = _resident((1, D_MODEL))
  return pl.pallas_call(
      _mix_out_body,
      grid=(t // OUT_ROWS,),
      in_specs=[row, row, pl.BlockSpec((CNV_TILES, OUT_ROWS, LANES), lambda i: (0, i, 0)),
                vec, _resident((D_MODEL, 2 * D_MODEL)), square, square, square,
                vec, vec, vec, vec, vec],
      out_specs=row,
      out_shape=jax.ShapeDtypeStruct((t, D_MODEL), _F32),
      compiler_params=pltpu.CompilerParams(
          dimension_semantics=("arbitrary",), vmem_limit_bytes=VMEM_LIMIT_BYTES),
      name="mix_out",
  )(x2, og, cv, npre, wgat, wgo, wmo, cwo, dwb, lng, lnb, cbo, npost)


def _row(v):
  return v.reshape(1, -1).astype(_F32)


def _lane_tiles(w):
  k, c = w.shape
  tiles = w.reshape(k, c // LANES, LANES).transpose(1, 0, 2).astype(_F32)
  return jnp.broadcast_to(tiles[:, :, None, :], (c // LANES, k, SUBLANES, LANES))


def kernel(x, ffn1_norm_pre, ffn1_norm_post, ffn1_w_in, ffn1_w_out, mix_norm_pre, mix_norm_post,
           mix_w_in, gdn_conv_w, gdn_a_log, gdn_dt_bias, gdn_norm_w, gdn_w_o, cnv_pw1_b, cnv_dw_w,
           cnv_dw_b, cnv_ln_g, cnv_ln_b, cnv_w_o, cnv_b_o, mix_w_out, ffn2_norm_pre,
           ffn2_norm_post, ffn2_w_in, ffn2_w_out):
  batch, seq, _ = x.shape
  depth = ffn1_w_in.shape[0]
  x2 = x.reshape(batch * seq, D_MODEL)
  o_qkv = 3 * HEADS * HEAD_DIM
  o_z = o_qkv + HEADS * HEAD_DIM
  o_b = o_z + HEADS
  o_a = o_b + HEADS
  o_glu = o_a + 2 * CNV_CH
  ffn1_in, ffn1_out = ffn1_w_in.astype(_BF16), ffn1_w_out.astype(_BF16)
  ffn2_in, ffn2_out = ffn2_w_in.astype(_BF16), ffn2_w_out.astype(_BF16)
  mix_in_w = mix_w_in.astype(_BF16)
  gdn_out, cnv_out, mix_out_w = (gdn_w_o.astype(_BF16), cnv_w_o.astype(_BF16),
                                 mix_w_out.astype(_BF16))
  for i in range(depth):
    x2 = _ffn(x2, _row(ffn1_norm_pre[i]), _row(ffn1_norm_post[i]), ffn1_in, ffn1_out, i)

    w_beta = mix_in_w[i, :, o_z:o_b]
    w_a = mix_in_w[i, :, o_b:o_a]
    wsm = jnp.concatenate(
        [w_beta, w_a, w_a, jnp.zeros((D_MODEL, SMALL_COLS - 3 * HEADS), _BF16)], axis=1)
    wsmt = jnp.concatenate(
        [w_beta, w_a, w_a, jnp.zeros((D_MODEL, SMALL_ROWS - 3 * HEADS), _BF16)], axis=1).T
    pad = jnp.zeros((HEADS,), _F32)
    a_log = jnp.concatenate([pad, gdn_a_log[i], gdn_a_log[i]])
    dt_b = jnp.concatenate([pad, gdn_dt_bias[i], gdn_dt_bias[i]])
    smc = jnp.zeros((SUBLANES, SMALL_COLS), _F32)
    smc = smc.at[0, :3 * HEADS].set(a_log).at[1, :3 * HEADS].set(dt_b)
    smct = jnp.zeros((SMALL_ROWS, LANES), _F32)
    smct = smct.at[:3 * HEADS, 0].set(a_log).at[:3 * HEADS, 1].set(dt_b)

    npre = _row(mix_norm_pre[i])
    proj, cv, bg, bgt = _mix_in(
        x2, npre, mix_in_w, mix_in_w[i, :, o_a:o_glu], wsm, wsmt, smc, smct,
        _row(cnv_pw1_b[i]), _lane_tiles(cnv_dw_w[i]), i, batch, seq)
    og = _gdn(proj, bg, bgt, _row(gdn_norm_w[i]), _lane_tiles(gdn_conv_w[i]), batch, seq)
    x2 = _mix_out(x2, og, cv, npre, mix_in_w[i, :, o_glu:], gdn_out, mix_out_w, cnv_out,
                  _row(cnv_dw_b[i]), _row(cnv_ln_g[i]), _row(cnv_ln_b[i]), _row(cnv_b_o[i]),
                  _row(mix_norm_post[i]), i)

    x2 = _ffn(x2, _row(ffn2_norm_pre[i]), _row(ffn2_norm_post[i]), ffn2_in, ffn2_out, i)
  return x2.reshape(batch, seq, D_MODEL)
```
